```python
import jax
import jax.numpy as jnp
from jax import lax
import numpy as np

D_MODEL = 4096
BATCH = 1
SEQ = 16384
DEPTH = 4

HEAD_DIM = 128
ROPE_DIM = HEAD_DIM // 4
ROPE_THETA = 500000.0
NORM_EPS = 1e-6
Q_BLOCK = 128
ATTN_SCALE = HEAD_DIM ** -0.5

NSA_HEADS = 16
NSA_KV_HEADS = 4
NSA_GROUP = NSA_HEADS // NSA_KV_HEADS
NSA_WIDTH = NSA_HEADS * HEAD_DIM
KV_WIDTH = NSA_KV_HEADS * HEAD_DIM
CMP_LEN = 32
CMP_STRIDE = 16
SLC_LEN = 64
SLC_TOPK = 16
WIN_LEN = 512

CONV_WIDTH = D_MODEL - NSA_WIDTH
CONV_K = 3

O_KV = NSA_WIDTH
O_GATE = O_KV + 6 * KV_WIDTH
O_CONV = O_GATE + 3 * NSA_HEADS
IN_A = O_CONV + 3 * CONV_WIDTH

FOX_HEADS = 24
FOX_WIDTH = FOX_HEADS * HEAD_DIM
POOL_WINDOWS = (2, 4, 8, 16)
POOL_WIDTH = D_MODEL - FOX_WIDTH
POOL_GROUP = POOL_WIDTH // len(POOL_WINDOWS)

O_FGATE = 3 * FOX_WIDTH
O_POOL = O_FGATE + FOX_HEADS
IN_B = O_POOL + POOL_WIDTH

FFN_HIDDEN = ((8 * D_MODEL + 3 * 256 - 1) // (3 * 256)) * 256
PLE_DIM = 256
N_EVEN = (DEPTH + 1) // 2
N_ODD = DEPTH // 2

kernel_name = 'hybrid_nsa_conv_fox_pool_trunk'


def rmsnorm(x, g):
    x32 = x.astype(jnp.float32)
    y = x32 * lax.rsqrt(jnp.mean(x32 * x32, axis=-1, keepdims=True) + NORM_EPS)
    return (y * g).astype(x.dtype)


def partial_rope(x, positions):
    half = ROPE_DIM // 2
    inv_freq = ROPE_THETA ** (-jnp.arange(half, dtype=jnp.float32) / half)
    ang = positions.astype(jnp.float32)[..., None] * inv_freq
    cos = jnp.cos(ang)[:, :, None, :]
    sin = jnp.sin(ang)[:, :, None, :]
    xr = x[..., :ROPE_DIM].astype(jnp.float32)
    x1, x2 = xr[..., :half], xr[..., half:]
    rot = jnp.concatenate([x1 * cos - x2 * sin, x2 * cos + x1 * sin], axis=-1).astype(x.dtype)
    return jnp.concatenate([rot, x[..., ROPE_DIM:]], axis=-1)


def masked_softmax(s, mask):
    s = jnp.where(mask, s.astype(jnp.float32), -jnp.inf)
    m = jnp.max(s, axis=-1, keepdims=True)
    m = jnp.where(jnp.isfinite(m), m, 0.0)
    e = jnp.exp(s - m)
    return e / jnp.maximum(jnp.sum(e, axis=-1, keepdims=True), 1e-30)


def compress(kv, pe, w1, w2):
    B, S, G, Dh = kv.shape
    R = CMP_LEN // CMP_STRIDE
    n_chunk = S // CMP_STRIDE
    n_cmp = n_chunk - R + 1
    chunks = kv.reshape(B, n_chunk, CMP_STRIDE, G, Dh)
    pe = pe.reshape(R, CMP_STRIDE, 1, Dh)
    w1 = w1.reshape(R, CMP_STRIDE, Dh, Dh)
    hid = sum(jnp.einsum('bnlgd,lde->bnge', chunks[:, r:r + n_cmp] + pe[r], w1[r]) for r in range(R))
    return jax.nn.gelu(hid) @ w2


def nsa_attention(q, k_cmp, v_cmp, k_slc, v_slc, k_win, v_win, gates):
    B, S, H, Dh = q.shape
    G = NSA_KV_HEADS
    n_cmp = k_cmp.shape[1]
    n_slc = S // SLC_LEN
    topk = min(SLC_TOPK, n_slc)
    dt = v_slc.dtype
    c_start = jnp.arange(n_cmp) * CMP_STRIDE
    c_last = c_start + CMP_LEN - 1
    s_start = jnp.arange(n_slc) * SLC_LEN
    overlap = jnp.clip(jnp.minimum(c_start[:, None] + CMP_LEN, s_start[None, :] + SLC_LEN)
                       - jnp.maximum(c_start[:, None], s_start[None, :]), 0, None).astype(jnp.float32) / CMP_LEN
    blk = jnp.arange(n_slc)
    kb_slc = k_slc.reshape(B, n_slc, SLC_LEN, G, Dh).transpose(0, 3, 1, 2, 4)
    vb_slc = v_slc.reshape(B, n_slc, SLC_LEN, G, Dh).transpose(0, 3, 1, 2, 4)
    bi = jnp.arange(B)[:, None, None, None]
    gi = jnp.arange(G)[None, :, None, None]
    pad = ((0, 0), (WIN_LEN, 0), (0, 0), (0, 0))
    kw_pad = jnp.pad(k_win, pad)
    vw_pad = jnp.pad(v_win, pad)

    def block(qb):
        t0 = qb * Q_BLOCK
        tpos = t0 + jnp.arange(Q_BLOCK)
        qg = lax.dynamic_slice_in_dim(q, t0, Q_BLOCK, axis=1).reshape(B, Q_BLOCK, G, NSA_GROUP, Dh)
        gb = lax.dynamic_slice_in_dim(gates, t0, Q_BLOCK, axis=1).reshape(B, Q_BLOCK, G, NSA_GROUP, 3)
        s = jnp.einsum('bqgjd,bngd->bgjqn', qg, k_cmp, preferred_element_type=jnp.float32) * ATTN_SCALE
        p_cmp = masked_softmax(s, c_last[None, :] <= tpos[:, None])
        o_cmp = jnp.einsum('bgjqn,bngd->bqgjd', p_cmp.astype(dt), v_cmp)
        imp = jnp.einsum('bgjqn,nm->bgqm', p_cmp, overlap)
        cur = tpos // SLC_LEN
        forced = (blk[None, :] == 0) | (blk[None, :] == cur[:, None]) | (blk[None, :] == cur[:, None] - 1)
        valid = s_start[None, :] <= tpos[:, None]
        imp = jnp.where(forced, jnp.inf, jnp.where(valid, imp, -jnp.inf))
        _, sel = lax.top_k(imp, topk)
        ks = kb_slc[bi, gi, sel].reshape(B, G, Q_BLOCK, topk * SLC_LEN, Dh)
        vs = vb_slc[bi, gi, sel].reshape(B, G, Q_BLOCK, topk * SLC_LEN, Dh)
        kpos = (sel[..., None] * SLC_LEN + jnp.arange(SLC_LEN)).reshape(B, G, Q_BLOCK, topk * SLC_LEN)
        s = jnp.einsum('bqgjd,bgqnd->bgjqn', qg, ks, preferred_element_type=jnp.float32) * ATTN_SCALE
        p_slc = masked_softmax(s, (kpos <= tpos[None, None, :, None])[:, :, None])
        o_slc = jnp.einsum('bgjqn,bgqnd->bqgjd', p_slc.astype(dt), vs)
        kw = lax.dynamic_slice_in_dim(kw_pad, t0, WIN_LEN + Q_BLOCK, axis=1)
        vw = lax.dynamic_slice_in_dim(vw_pad, t0, WIN_LEN + Q_BLOCK, axis=1)
        wpos = t0 - WIN_LEN + jnp.arange(WIN_LEN + Q_BLOCK)
        wmask = (wpos[None, :] <= tpos[:, None]) & (wpos[None, :] > tpos[:, None] - WIN_LEN) & (wpos[None, :] >= 0)
        s = jnp.einsum('bqgjd,bngd->bgjqn', qg, kw, preferred_element_type=jnp.float32) * ATTN_SCALE
        p_win = masked_softmax(s, wmask)
        o_win = jnp.einsum('bgjqn,bngd->bqgjd', p_win.astype(dt), vw)
        o = gb[..., 0:1] * o_cmp + gb[..., 1:2] * o_slc + gb[..., 2:3] * o_win
        return o.reshape(B, Q_BLOCK, H * Dh)

    out = lax.map(block, jnp.arange(S // Q_BLOCK))
    return out.transpose(1, 0, 2, 3).reshape(B, S, H * Dh)


def short_conv(b_gate, c_gate, h, w):
    u = c_gate * h
    y = lax.conv_general_dilated(u, w[:, None, :].astype(u.dtype), window_strides=(1,),
                                 padding=[(CONV_K - 1, 0)], dimension_numbers=('NWC', 'WIO', 'NWC'),
                                 feature_group_count=u.shape[-1])
    return b_gate * y


def even_mixer(h, positions, w_in, q_norm, k_norm, cmp_pe, cmp_w1, cmp_w2, conv_w, w_out):
    B, S, _ = h.shape
    z = h @ w_in
    q = z[..., :O_KV].reshape(B, S, NSA_HEADS, HEAD_DIM)
    kv = z[..., O_KV:O_GATE].reshape(B, S, 6, NSA_KV_HEADS, HEAD_DIM)
    gates = jax.nn.sigmoid(z[..., O_GATE:O_CONV].reshape(B, S, NSA_HEADS, 3))
    bc = z[..., O_CONV:].reshape(B, S, 3, CONV_WIDTH)
    q = partial_rope(rmsnorm(q, q_norm), positions)
    k_cmp = rmsnorm(compress(partial_rope(kv[:, :, 0], positions), cmp_pe[0], cmp_w1[0], cmp_w2[0]), k_norm[0])
    v_cmp = compress(kv[:, :, 1], cmp_pe[1], cmp_w1[1], cmp_w2[1])
    k_slc = partial_rope(rmsnorm(kv[:, :, 2], k_norm[1]), positions)
    k_win = partial_rope(rmsnorm(kv[:, :, 4], k_norm[2]), positions)
    o_nsa = nsa_attention(q, k_cmp, v_cmp, k_slc, kv[:, :, 3], k_win, kv[:, :, 5], gates)
    o_conv = short_conv(bc[:, :, 0], bc[:, :, 1], bc[:, :, 2], conv_w)
    return jnp.concatenate([o_nsa, o_conv], axis=-1) @ w_out


def fox_attention(q, k, v, log_f):
    B, S, H, Dh = q.shape
    c = jnp.cumsum(log_f, axis=1).transpose(0, 2, 1)
    kpos = jnp.arange(S)

    def block(qb):
        t0 = qb * Q_BLOCK
        tpos = t0 + jnp.arange(Q_BLOCK)
        qblk = lax.dynamic_slice_in_dim(q, t0, Q_BLOCK, axis=1)
        c_q = lax.dynamic_slice_in_dim(c, t0, Q_BLOCK, axis=2)
        s = jnp.einsum('bqhd,bkhd->bhqk', qblk, k, preferred_element_type=jnp.float32) * ATTN_SCALE
        s = s + c_q[..., None] - c[:, :, None, :]
        s = jnp.where(kpos[None, :] <= tpos[:, None], s, -jnp.inf)
        prob = jax.nn.softmax(s, axis=-1).astype(v.dtype)
        return jnp.einsum('bhqk,bkhd->bqhd', prob, v)

    out = lax.map(block, jnp.arange(S // Q_BLOCK))
    return out.transpose(1, 0, 2, 3, 4).reshape(B, S, H * Dh)


def multiscale_pool(u, w_pool, scale):
    B, S, _ = u.shape
    u32 = u.astype(jnp.float32).reshape(B, S, len(POOL_WINDOWS), POOL_GROUP)
    cs = jnp.cumsum(u32, axis=1)
    t = jnp.arange(S)
    diffs = []
    for g, w in enumerate(POOL_WINDOWS):
        hi = cs[:, :, g]
        lo = jnp.pad(hi, ((0, 0), (w, 0), (0, 0)))[:, :S]
        cnt = jnp.minimum(t + 1, w).astype(jnp.float32)[None, :, None]
        diffs.append((hi - lo) / cnt - u32[:, :, g])
    d = jnp.stack(diffs, axis=2).astype(u.dtype)
    y = jnp.einsum('bsgc,gcd->bsgd', d, w_pool).reshape(B, S, POOL_WIDTH)
    return y * scale


def odd_mixer(h, w_in, f_bias, q_norm, k_norm, pool_w, pool_scale, w_out):
    B, S, _ = h.shape
    z = h @ w_in
    qkv = z[..., :O_FGATE].reshape(B, S, 3, FOX_HEADS, HEAD_DIM)
    log_f = jax.nn.log_sigmoid((z[..., O_FGATE:O_POOL] + f_bias).astype(jnp.float32))
    u = z[..., O_POOL:]
    q = rmsnorm(qkv[:, :, 0], q_norm)
    k = rmsnorm(qkv[:, :, 1], k_norm)
    o_fox = fox_attention(q, k, qkv[:, :, 2], log_f)
    o_pool = multiscale_pool(u, pool_w, pool_scale)
    return jnp.concatenate([o_fox, o_pool], axis=-1) @ w_out


def swiglu(h, w1, w3, w2):
    return (jax.nn.silu(h @ w1) * (h @ w3)) @ w2


def setup_inputs(seed: int = 0) -> dict:
    key = jax.random.key(seed)
    ks = iter(jax.random.split(key, 40))

    def nrm(shape, scale):
        return jax.random.normal(next(ks), shape, jnp.float32) * scale

    def gain(shape):
        return 1.0 + 0.05 * jax.random.normal(next(ks), shape, jnp.float32)

    d_s = D_MODEL ** -0.5
    return {
        'x': nrm((BATCH, SEQ, D_MODEL), 1.0),
        'p': nrm((DEPTH, BATCH, SEQ, PLE_DIM), 1.0),
        'positions': jnp.broadcast_to(jnp.arange(SEQ, dtype=jnp.int32), (BATCH, SEQ)),
        'a_norm': gain((N_EVEN, D_MODEL)),
        'a_w_in': nrm((N_EVEN, D_MODEL, IN_A), d_s),
        'a_q_norm': gain((N_EVEN, HEAD_DIM)),
        'a_k_norm': gain((N_EVEN, 3, HEAD_DIM)),
        'a_cmp_pe': nrm((N_EVEN, 2, CMP_LEN, HEAD_DIM), 0.1),
        'a_cmp_w1': nrm((N_EVEN, 2, CMP_LEN * HEAD_DIM, HEAD_DIM), (CMP_LEN * HEAD_DIM) ** -0.5),
        'a_cmp_w2': nrm((N_EVEN, 2, HEAD_DIM, HEAD_DIM), HEAD_DIM ** -0.5),
        'a_conv_w': nrm((N_EVEN, CONV_K, CONV_WIDTH), CONV_K ** -0.5),
        'a_w_out': nrm((N_EVEN, D_MODEL, D_MODEL), d_s),
        'b_norm': gain((N_ODD, D_MODEL)),
        'b_w_in': nrm((N_ODD, D_MODEL, IN_B), d_s),
        'b_f_bias': 2.0 + nrm((N_ODD, FOX_HEADS), 0.5),
        'b_q_norm': gain((N_ODD, HEAD_DIM)),
        'b_k_norm': gain((N_ODD, HEAD_DIM)),
        'b_pool_w': nrm((N_ODD, len(POOL_WINDOWS), POOL_GROUP, POOL_GROUP), POOL_GROUP ** -0.5),
        'b_pool_scale': 1.0 + nrm((N_ODD, POOL_WIDTH), 0.1),
        'b_w_out': nrm((N_ODD, D_MODEL, D_MODEL), d_s),
        'f_norm': gain((DEPTH, D_MODEL)),
        'f_w1': nrm((DEPTH, D_MODEL, FFN_HIDDEN), d_s),
        'f_w3': nrm((DEPTH, D_MODEL, FFN_HIDDEN), d_s),
        'f_w2': nrm((DEPTH, FFN_HIDDEN, D_MODEL), FFN_HIDDEN ** -0.5),
        'e_norm': gain((DEPTH, D_MODEL)),
        'e_w_gate': nrm((DEPTH, D_MODEL, D_MODEL), d_s),
        'e_w_proj': nrm((DEPTH, PLE_DIM, D_MODEL), PLE_DIM ** -0.5),
    }


def reference(x, p, positions, a_norm, a_w_in, a_q_norm, a_k_norm, a_cmp_pe, a_cmp_w1, a_cmp_w2,
              a_conv_w, a_w_out, b_norm, b_w_in, b_f_bias, b_q_norm, b_k_norm, b_pool_w,
              b_pool_scale, b_w_out, f_norm, f_w1, f_w3, f_w2, e_norm, e_w_gate, e_w_proj):
    for i in range(DEPTH):
        j = i // 2
        if i % 2 == 0:
            x = x + even_mixer(rmsnorm(x, a_norm[j]), positions, a_w_in[j], a_q_norm[j], a_k_norm[j],
                               a_cmp_pe[j], a_cmp_w1[j], a_cmp_w2[j], a_conv_w[j], a_w_out[j])
        else:
            x = x + odd_mixer(rmsnorm(x, b_norm[j]), b_w_in[j], b_f_bias[j], b_q_norm[j], b_k_norm[j],
                              b_pool_w[j], b_pool_scale[j], b_w_out[j])
        x = x + swiglu(rmsnorm(x, f_norm[i]), f_w1[i], f_w3[i], f_w2[i])
        gate = jax.nn.sigmoid(rmsnorm(x, e_norm[i]) @ e_w_gate[i])
        x = x + gate * (p[i] @ e_w_proj[i])
    return x
```

```python
import functools

import jax
import jax.numpy as jnp
import numpy as np
from jax import lax
from jax.experimental import pallas as pl
from jax.experimental.pallas import tpu as pltpu

HEAD_DIM = 128
ROPE_DIM = HEAD_DIM // 4
ROPE_HALF = ROPE_DIM // 2
ROPE_THETA = 500000.0
NORM_EPS = 1e-6
ATTN_SCALE = HEAD_DIM ** -0.5
Q_BLOCK = 128

NSA_HEADS = 16
NSA_KV_HEADS = 4
NSA_GROUP = NSA_HEADS // NSA_KV_HEADS
NSA_WIDTH = NSA_HEADS * HEAD_DIM
KV_WIDTH = NSA_KV_HEADS * HEAD_DIM
CMP_LEN = 32
CMP_STRIDE = 16
SLC_LEN = 64
SLC_TOPK = 16
WIN_LEN = 512
CONV_K = 3

FOX_HEADS = 24
FOX_WIDTH = FOX_HEADS * HEAD_DIM
POOL_WINDOWS = (2, 4, 8, 16)
POOL_GROUP = 256
POOL_HALO = 16
CONV_HALO = 8

LANES = 128
VMEM_LIMIT_BYTES = 56 * 1024 * 1024
BF16 = jnp.bfloat16
F32 = jnp.float32
NEG_INF = float("-inf")


def _params(sem):
    return pltpu.CompilerParams(dimension_semantics=sem, vmem_limit_bytes=VMEM_LIMIT_BYTES)


def _tile(dim, pref):
    if dim <= pref:
        return dim
    t = pref
    while dim % t:
        t //= 2
    return t


def _rmsnorm_kernel(x_ref, g_ref, o_ref):
    x = x_ref[...]
    ms = jnp.mean(x * x, axis=-1, keepdims=True)
    o_ref[...] = (x * lax.rsqrt(ms + NORM_EPS) * g_ref[...]).astype(o_ref.dtype)


def rmsnorm_bf16(x, g):
    s, d = x.shape
    tm = _tile(s, 256)
    return pl.pallas_call(
        _rmsnorm_kernel,
        grid=(s // tm,),
        in_specs=[pl.BlockSpec((tm, d), lambda i: (i, 0)), pl.BlockSpec((1, d), lambda i: (0, 0))],
        out_specs=pl.BlockSpec((tm, d), lambda i: (i, 0)),
        out_shape=jax.ShapeDtypeStruct((s, d), BF16),
        compiler_params=_params(("parallel",)),
        name="rmsnorm",
    )(x, g.reshape(1, d))


def _mm_kernel(*refs, nk, n_extra, epilogue):
    a_ref, b_ref = refs[0], refs[1]
    extra = refs[2:2 + n_extra]
    o_ref = refs[2 + n_extra]
    part = jnp.dot(a_ref[...], b_ref[...], preferred_element_type=F32)
    if nk == 1:
        o_ref[...] = epilogue(part, *extra).astype(o_ref.dtype)
        return
    acc_ref = refs[3 + n_extra]
    k = pl.program_id(2)

    @pl.when(k == 0)
    def _():
        acc_ref[...] = part

    @pl.when(k > 0)
    def _():
        acc_ref[...] += part

    @pl.when(k == nk - 1)
    def _():
        o_ref[...] = epilogue(acc_ref[...], *extra).astype(o_ref.dtype)


def matmul(a, b, *, out_dtype, epilogue=None, extras=(), tm=1024, tn=512, tk=4096, name="matmul"):
    m, kd = a.shape
    n = b.shape[1]
    tm, tn, tk = _tile(m, tm), _tile(n, tn), _tile(kd, tk)
    nk = kd // tk
    if epilogue is None:
        epilogue = lambda acc: acc
    in_specs = [pl.BlockSpec((tm, tk), lambda i, j, k: (i, k)), pl.BlockSpec((tk, tn), lambda i, j, k: (k, j))]
    in_specs += [pl.BlockSpec(bs, im) for _, bs, im in extras]
    scratch = [pltpu.VMEM((tm, tn), F32)] if nk > 1 else []
    return pl.pallas_call(
        functools.partial(_mm_kernel, nk=nk, n_extra=len(extras), epilogue=epilogue),
        grid=(m // tm, n // tn, nk),
        in_specs=in_specs,
        out_specs=pl.BlockSpec((tm, tn), lambda i, j, k: (i, j)),
        out_shape=jax.ShapeDtypeStruct((m, n), out_dtype),
        scratch_shapes=scratch,
        compiler_params=_params(("parallel", "parallel", "arbitrary")),
        name=name,
    )(a, b, *[e[0] for e in extras])


def _row_extra(arr, tn):
    return (arr, (1, tn), lambda i, j, k: (0, j))


def _tile_extra(arr, tm, tn):
    return (arr, (tm, tn), lambda i, j, k: (i, j))


def _rope_extras(rope, tm):
    return [(t, (tm, HEAD_DIM), lambda i, j, k: (i, 0)) for t in rope]


def _head_epilogue(acc, *refs, norm, rope, scale):
    refs = list(refs)
    g_ref = refs.pop(0) if norm else None
    if rope:
        cos_ref, sa_ref, sb_ref = refs
    outs = []
    for c in range(acc.shape[1] // HEAD_DIM):
        blk = acc[:, c * HEAD_DIM:(c + 1) * HEAD_DIM]
        if norm:
            ms = jnp.mean(blk * blk, axis=-1, keepdims=True)
            blk = blk * lax.rsqrt(ms + NORM_EPS) * g_ref[:, c * HEAD_DIM:(c + 1) * HEAD_DIM]
        if rope:
            blk = (blk * cos_ref[...] + pltpu.roll(blk, HEAD_DIM - ROPE_HALF, 1) * sa_ref[...]
                   + pltpu.roll(blk, ROPE_HALF, 1) * sb_ref[...])
        if scale != 1.0:
            blk = blk * scale
        outs.append(blk)
    return jnp.concatenate(outs, axis=1) if len(outs) > 1 else outs[0]


def head_proj(h, w, *, gain=None, rope=None, scale=1.0, name):
    m = h.shape[0]
    n = w.shape[1]
    tm, tn = _tile(m, 1024), _tile(n, 512)
    extras = []
    if gain is not None:
        extras.append(_row_extra(gain.reshape(1, n), tn))
    if rope is not None:
        extras += _rope_extras(rope, tm)
    ep = functools.partial(_head_epilogue, norm=gain is not None, rope=rope is not None, scale=scale)
    return matmul(h, w, out_dtype=BF16, epilogue=ep, extras=extras, tm=tm, tn=tn, name=name)


def _rope_table_kernel(pos_ref, freq_ref, cos_ref, sa_ref, sb_ref):
    ang = pos_ref[...] * freq_ref[...]
    lane = lax.broadcasted_iota(jnp.int32, ang.shape, 1)
    c, s = jnp.cos(ang), jnp.sin(ang)
    cos_ref[...] = jnp.where(lane < ROPE_DIM, c, 1.0)
    sa_ref[...] = jnp.where(lane < ROPE_HALF, -s, 0.0)
    sb_ref[...] = jnp.where((lane >= ROPE_HALF) & (lane < ROPE_DIM), s, 0.0)


def rope_tables(positions):
    s = positions.shape[0]
    inv_freq = ROPE_THETA ** (-jnp.arange(ROPE_HALF, dtype=F32) / ROPE_HALF)
    freq_row = jnp.zeros((HEAD_DIM,), F32).at[:ROPE_DIM].set(jnp.concatenate([inv_freq, inv_freq]))
    pos_rep = jnp.broadcast_to(positions.astype(F32)[:, None], (s, HEAD_DIM))
    tm = _tile(s, 1024)
    spec = pl.BlockSpec((tm, HEAD_DIM), lambda i: (i, 0))
    shp = jax.ShapeDtypeStruct((s, HEAD_DIM), F32)
    return pl.pallas_call(
        _rope_table_kernel,
        grid=(s // tm,),
        in_specs=[spec, pl.BlockSpec((1, HEAD_DIM), lambda i: (0, 0))],
        out_specs=[spec, spec, spec],
        out_shape=[shp, shp, shp],
        compiler_params=_params(("parallel",)),
        name="rope_tables",
    )(pos_rep, freq_row.reshape(1, HEAD_DIM))


def _gelu_tanh(x):
    return 0.5 * x * (1.0 + jnp.tanh(np.sqrt(2.0 / np.pi).astype(np.float32) * (x + 0.044715 * (x * x * x))))


def _compress_kernel(a_ref, w_ref, pe_ref, w2_ref, g_ref, o_ref, acc_ref, bias_ref, *, norm):
    l = pl.program_id(0)
    nl = pl.num_programs(0)
    a = a_ref[...]
    w = w_ref[0]
    pe_part = jnp.dot(pe_ref[0], w, preferred_element_type=F32)

    @pl.when(l == 0)
    def _():
        bias_ref[...] = pe_part
        for g in range(NSA_KV_HEADS):
            acc_ref[g] = jnp.dot(a[:, g * HEAD_DIM:(g + 1) * HEAD_DIM], w, preferred_element_type=F32)

    @pl.when(l > 0)
    def _():
        bias_ref[...] += pe_part
        for g in range(NSA_KV_HEADS):
            acc_ref[g] += jnp.dot(a[:, g * HEAD_DIM:(g + 1) * HEAD_DIM], w, preferred_element_type=F32)

    @pl.when(l == nl - 1)
    def _():
        n_chunk = a.shape[0]
        bias = bias_ref[0:1, :HEAD_DIM] + bias_ref[1:2, HEAD_DIM:]
        for g in range(NSA_KV_HEADS):
            p = acc_ref[g]
            hid = p[:, :HEAD_DIM] + pltpu.roll(p[:, HEAD_DIM:], n_chunk - 1, 0) + bias
            out = jnp.dot(_gelu_tanh(hid).astype(BF16), w2_ref[...], preferred_element_type=F32)
            if norm:
                ms = jnp.mean(out * out, axis=-1, keepdims=True)
                out = out * lax.rsqrt(ms + NORM_EPS) * g_ref[...]
            o_ref[:, g * HEAD_DIM:(g + 1) * HEAD_DIM] = out.astype(o_ref.dtype)


def compress(kv, pe, w1, w2, gain):
    s = kv.shape[0]
    n_chunk = s // CMP_STRIDE
    a = kv.reshape(n_chunk, CMP_STRIDE * KV_WIDTH)
    r = CMP_LEN // CMP_STRIDE
    w1r = w1.reshape(r, CMP_STRIDE, HEAD_DIM, HEAD_DIM)
    wcat = jnp.concatenate([w1r[0], w1r[1]], axis=-1).astype(BF16)
    pe_r = pe.reshape(r, CMP_STRIDE, HEAD_DIM).transpose(1, 0, 2)
    pe_l = jnp.zeros((CMP_STRIDE, 8, HEAD_DIM), F32).at[:, :r].set(pe_r).astype(BF16)
    norm = gain is not None
    g = (gain if norm else jnp.ones((HEAD_DIM,), F32)).reshape(1, HEAD_DIM)
    return pl.pallas_call(
        functools.partial(_compress_kernel, norm=norm),
        grid=(CMP_STRIDE,),
        in_specs=[
            pl.BlockSpec((n_chunk, KV_WIDTH), lambda l: (0, l)),
            pl.BlockSpec((1, HEAD_DIM, 2 * HEAD_DIM), lambda l: (l, 0, 0)),
            pl.BlockSpec((1, 8, HEAD_DIM), lambda l: (l, 0, 0)),
            pl.BlockSpec((HEAD_DIM, HEAD_DIM), lambda l: (0, 0)),
            pl.BlockSpec((1, HEAD_DIM), lambda l: (0, 0)),
        ],
        out_specs=pl.BlockSpec((n_chunk, KV_WIDTH), lambda l: (0, 0)),
        out_shape=jax.ShapeDtypeStruct((n_chunk, KV_WIDTH), BF16),
        scratch_shapes=[pltpu.VMEM((NSA_KV_HEADS, n_chunk, 2 * HEAD_DIM), F32), pltpu.VMEM((8, 2 * HEAD_DIM), F32)],
        compiler_params=_params(("arbitrary",)),
        name="nsa_compress",
    )(a, wcat, pe_l, w2.astype(BF16), g)


def _stack_heads(qb):
    return jnp.concatenate([qb[:, j * HEAD_DIM:(j + 1) * HEAD_DIM] for j in range(NSA_GROUP)], axis=0)


def _cmp_kernel(q_ref, k_ref, v_ref, ov_ref, o_ref, sel_ref):
    i = pl.program_id(1)
    tq = q_ref.shape[0]
    n_chunk = k_ref.shape[0]
    nb = sel_ref.shape[-1]
    q4 = _stack_heads(q_ref[...])
    s = lax.dot_general(q4, k_ref[...], (((1,), (1,)), ((), ())), preferred_element_type=F32)
    rows = lax.broadcasted_iota(jnp.int32, (NSA_GROUP * tq, n_chunk), 0)
    cols = lax.broadcasted_iota(jnp.int32, (NSA_GROUP * tq, n_chunk), 1)
    tpos = i * tq + rows % tq
    s = jnp.where(cols * CMP_STRIDE + (CMP_LEN - 1) <= tpos, s, NEG_INF)
    m = jnp.max(s, axis=-1, keepdims=True)
    m = jnp.where(m == NEG_INF, 0.0, m)
    e = jnp.exp(s - m)
    p = e * (1.0 / jnp.maximum(jnp.sum(e, axis=-1, keepdims=True), 1e-30))
    o = jnp.dot(p.astype(BF16), v_ref[...], preferred_element_type=F32)
    for j in range(NSA_GROUP):
        o_ref[:, j * HEAD_DIM:(j + 1) * HEAD_DIM] = o[j * tq:(j + 1) * tq]

    psum = p[0:tq] + p[tq:2 * tq] + p[2 * tq:3 * tq] + p[3 * tq:4 * tq]
    p_hi = psum.astype(BF16)
    p_lo = (psum - p_hi.astype(F32)).astype(BF16)
    ov = ov_ref[...]
    imp = jnp.dot(p_hi, ov, preferred_element_type=F32) + jnp.dot(p_lo, ov, preferred_element_type=F32)

    blk = lax.broadcasted_iota(jnp.int32, (tq, nb), 1)
    t = i * tq + lax.broadcasted_iota(jnp.int32, (tq, nb), 0)
    cur = t // SLC_LEN
    forced = (blk == 0) | (blk == cur) | (blk == cur - 1)
    valid = blk * SLC_LEN <= t
    work = jnp.where(forced, jnp.inf, jnp.where(valid, imp, NEG_INF))
    sel = jnp.zeros((tq, nb), F32)
    blk_f = blk.astype(F32)
    for _ in range(min(SLC_TOPK, nb)):
        mx = jnp.max(work, axis=-1, keepdims=True)
        first = jnp.min(jnp.where(work == mx, blk_f, float(nb)), axis=-1, keepdims=True)
        pick = (blk_f == first) & (mx > NEG_INF)
        sel = jnp.where(pick, 1.0, sel)
        work = jnp.where(pick, NEG_INF, work)
    sel_ref[...] = sel.astype(sel_ref.dtype)


def nsa_cmp_and_select(q, k_cmp, v_cmp):
    s = q.shape[0]
    n_chunk = k_cmp.shape[0]
    nb = s // SLC_LEN
    tq = Q_BLOCK
    c_start = np.arange(n_chunk) * CMP_STRIDE
    s_start = np.arange(nb) * SLC_LEN
    overlap = np.clip(np.minimum(c_start[:, None] + CMP_LEN, s_start[None, :] + SLC_LEN)
                      - np.maximum(c_start[:, None], s_start[None, :]), 0, None).astype(np.float32) / CMP_LEN
    gw = NSA_GROUP * HEAD_DIM
    return pl.pallas_call(
        _cmp_kernel,
        grid=(NSA_KV_HEADS, s // tq),
        in_specs=[
            pl.BlockSpec((tq, gw), lambda g, i: (i, g)),
            pl.BlockSpec((n_chunk, HEAD_DIM), lambda g, i: (0, g)),
            pl.BlockSpec((n_chunk, HEAD_DIM), lambda g, i: (0, g)),
            pl.BlockSpec((n_chunk, nb), lambda g, i: (0, 0)),
        ],
        out_specs=[
            pl.BlockSpec((tq, gw), lambda g, i: (i, g)),
            pl.BlockSpec((None, tq, nb), lambda g, i: (g, i, 0)),
        ],
        out_shape=[jax.ShapeDtypeStruct((s, NSA_WIDTH), F32), jax.ShapeDtypeStruct((NSA_KV_HEADS, s, nb), BF16)],
        compiler_params=_params(("parallel", "parallel")),
        name="nsa_cmp_select",
    )(q, k_cmp, v_cmp, jnp.asarray(overlap, BF16))


def _online_softmax_step(s, v, m_ref, l_ref, acc_ref):
    m_old = m_ref[...]
    m_new = jnp.maximum(m_old, jnp.max(s, axis=-1, keepdims=True))
    alpha = jnp.exp(m_old - m_new)
    p = jnp.exp(s - m_new)
    l_ref[...] = alpha * l_ref[...] + jnp.sum(p, axis=-1, keepdims=True)
    acc_ref[...] = alpha * acc_ref[...] + jnp.dot(p.astype(BF16), v, preferred_element_type=F32)
    m_ref[...] = m_new


def _init_softmax(m_ref, l_ref, acc_ref):
    m_ref[...] = jnp.full(m_ref.shape, NEG_INF, F32)
    l_ref[...] = jnp.zeros(l_ref.shape, F32)
    acc_ref[...] = jnp.zeros(acc_ref.shape, F32)


def _causal_pairs(n_q, tq, tk):
    qi, kj = [], []
    for i in range(n_q):
        last = (i * tq + tq - 1) // tk
        for j in range(last + 1):
            qi.append(i)
            kj.append(j)
    return jnp.asarray(qi, jnp.int32), jnp.asarray(kj, jnp.int32)


def _slc_kernel(qi_ref, kj_ref, q_ref, k_ref, v_ref, sel_ref, e_ref, o_ref, m_ref, l_ref, acc_ref):
    p_id = pl.program_id(1)
    i, j = qi_ref[p_id], kj_ref[p_id]
    tq, tk = q_ref.shape[0], k_ref.shape[0]

    @pl.when(j == 0)
    def _():
        _init_softmax(m_ref, l_ref, acc_ref)

    q4 = _stack_heads(q_ref[...])
    s = lax.dot_general(q4, k_ref[...], (((1,), (1,)), ((), ())), preferred_element_type=F32)
    selm = jnp.dot(sel_ref[...], e_ref[...], preferred_element_type=F32)
    selm = jnp.concatenate([selm] * NSA_GROUP, axis=0)
    rows = lax.broadcasted_iota(jnp.int32, s.shape, 0)
    cols = lax.broadcasted_iota(jnp.int32, s.shape, 1)
    ok = (selm > 0.5) & (j * tk + cols <= i * tq + rows % tq)
    s = jnp.where(ok, s, NEG_INF)
    _online_softmax_step(s, v_ref[...], m_ref, l_ref, acc_ref)

    @pl.when(j == (i * tq + tq - 1) // tk)
    def _():
        o = acc_ref[...] * (1.0 / l_ref[...])
        for h in range(NSA_GROUP):
            o_ref[:, h * HEAD_DIM:(h + 1) * HEAD_DIM] = o[h * tq:(h + 1) * tq]


def nsa_selected(q, k_slc, v_slc, sel):
    s = q.shape[0]
    nb = s // SLC_LEN
    tq, tk = Q_BLOCK, _tile(s, 512)
    qi, kj = _causal_pairs(s // tq, tq, tk)
    expand = (jnp.arange(s, dtype=jnp.int32)[None, :] // SLC_LEN == jnp.arange(nb, dtype=jnp.int32)[:, None]).astype(BF16)
    gw = NSA_GROUP * HEAD_DIM
    rows = NSA_GROUP * tq
    grid_spec = pltpu.PrefetchScalarGridSpec(
        num_scalar_prefetch=2,
        grid=(NSA_KV_HEADS, qi.shape[0]),
        in_specs=[
            pl.BlockSpec((tq, gw), lambda g, p, qi, kj: (qi[p], g)),
            pl.BlockSpec((tk, HEAD_DIM), lambda g, p, qi, kj: (kj[p], g)),
            pl.BlockSpec((tk, HEAD_DIM), lambda g, p, qi, kj: (kj[p], g)),
            pl.BlockSpec((None, tq, nb), lambda g, p, qi, kj: (g, qi[p], 0)),
            pl.BlockSpec((nb, tk), lambda g, p, qi, kj: (0, kj[p])),
        ],
        out_specs=pl.BlockSpec((tq, gw), lambda g, p, qi, kj: (qi[p], g)),
        scratch_shapes=[pltpu.VMEM((rows, 1), F32), pltpu.VMEM((rows, 1), F32), pltpu.VMEM((rows, HEAD_DIM), F32)],
    )
    return pl.pallas_call(
        _slc_kernel,
        grid_spec=grid_spec,
        out_shape=jax.ShapeDtypeStruct((s, NSA_WIDTH), F32),
        compiler_params=_params(("parallel", "arbitrary")),
        name="nsa_selected",
    )(qi, kj, q, k_slc, v_slc, sel, expand)


WIN_TILES = WIN_LEN // Q_BLOCK + 1


def _win_kernel(*refs):
    q_ref = refs[0]
    k_refs = refs[1:1 + WIN_TILES]
    v_refs = refs[1 + WIN_TILES:1 + 2 * WIN_TILES]
    ocmp_ref, oslc_ref, gate_ref, o_ref = refs[1 + 2 * WIN_TILES:]
    i = pl.program_id(0)
    tq = q_ref.shape[0]
    q4 = _stack_heads(q_ref[...])
    kcat = jnp.concatenate([r[...] for r in k_refs], axis=0)
    vcat = jnp.concatenate([r[...] for r in v_refs], axis=0)
    s = lax.dot_general(q4, kcat, (((1,), (1,)), ((), ())), preferred_element_type=F32)
    rows = lax.broadcasted_iota(jnp.int32, s.shape, 0)
    cols = lax.broadcasted_iota(jnp.int32, s.shape, 1)
    tpos = i * tq + rows % tq
    wpos = (i - (WIN_TILES - 1)) * tq + cols
    ok = (wpos <= tpos) & (wpos > tpos - WIN_LEN) & (wpos >= 0)
    s = jnp.where(ok, s, NEG_INF)
    m = jnp.max(s, axis=-1, keepdims=True)
    e = jnp.exp(s - m)
    p = e * (1.0 / jnp.sum(e, axis=-1, keepdims=True))
    o_win = jnp.dot(p.astype(BF16), vcat, preferred_element_type=F32)
    gates = gate_ref[...]
    for h in range(NSA_GROUP):
        sl = slice(h * HEAD_DIM, (h + 1) * HEAD_DIM)
        o = (gates[:, 3 * h:3 * h + 1] * ocmp_ref[:, sl] + gates[:, 3 * h + 1:3 * h + 2] * oslc_ref[:, sl]
             + gates[:, 3 * h + 2:3 * h + 3] * o_win[h * tq:(h + 1) * tq])
        o_ref[:, sl] = o.astype(o_ref.dtype)


def nsa_window_combine(q, k_win, v_win, o_cmp, o_slc, gates):
    s = q.shape[0]
    tq = Q_BLOCK
    gw = NSA_GROUP * HEAD_DIM
    back = WIN_TILES - 1
    kv_specs = [pl.BlockSpec((tq, HEAD_DIM), lambda i, g, d=d: (jnp.maximum(i - back + d, 0), g))
                for d in range(WIN_TILES)]
    blk = pl.BlockSpec((tq, gw), lambda i, g: (i, g))
    return pl.pallas_call(
        _win_kernel,
        grid=(s // tq, NSA_KV_HEADS),
        in_specs=[blk] + kv_specs + kv_specs + [blk, blk, pl.BlockSpec((None, tq, 3 * NSA_GROUP), lambda i, g: (g, i, 0))],
        out_specs=blk,
        out_shape=jax.ShapeDtypeStruct((s, NSA_WIDTH), BF16),
        compiler_params=_params(("parallel", "parallel")),
        name="nsa_window_combine",
    )(q, *([k_win] * WIN_TILES), *([v_win] * WIN_TILES), o_cmp, o_slc, gates)


def _conv_kernel(b_ref, c_ref, h_ref, ch_ref, hh_ref, w_ref, o_ref):
    i = pl.program_id(0)
    u = c_ref[...] * h_ref[...]
    halo = jnp.where(i > 0, ch_ref[...] * hh_ref[...], 0.0)
    x = jnp.concatenate([halo, u], axis=0)
    w = w_ref[...]
    y = (w[2:3] * x + w[1:2] * pltpu.roll(x, 1, 0) + w[0:1] * pltpu.roll(x, 2, 0))[CONV_HALO:]
    o_ref[...] = (b_ref[...] * y).astype(o_ref.dtype)


def short_conv(z, conv_w):
    s = z.shape[0]
    cw = conv_w.shape[1]
    tm, tc = _tile(s, 512), _tile(cw, 512)
    nc = cw // tc
    hb = tm // CONV_HALO
    halo = lambda off: pl.BlockSpec((CONV_HALO, tc), lambda i, j: (jnp.maximum(i * hb - 1, 0), off * nc + j))
    main = lambda off: pl.BlockSpec((tm, tc), lambda i, j: (i, off * nc + j))
    w8 = jnp.zeros((8, cw), F32).at[:CONV_K].set(conv_w)
    return pl.pallas_call(
        _conv_kernel,
        grid=(s // tm, nc),
        in_specs=[main(0), main(1), main(2), halo(1), halo(2), pl.BlockSpec((8, tc), lambda i, j: (0, j))],
        out_specs=pl.BlockSpec((tm, tc), lambda i, j: (i, j)),
        out_shape=jax.ShapeDtypeStruct((s, cw), BF16),
        compiler_params=_params(("parallel", "parallel")),
        name="short_conv",
    )(z, z, z, z, z, w8)


def _cumsum_kernel(x_ref, o_ref, carry_ref):
    @pl.when(pl.program_id(0) == 0)
    def _():
        carry_ref[...] = jnp.zeros(carry_ref.shape, F32)

    x = x_ref[...]
    t = x.shape[0]
    tri = (lax.broadcasted_iota(jnp.int32, (t, t), 0) >= lax.broadcasted_iota(jnp.int32, (t, t), 1)).astype(BF16)
    hi = x.astype(BF16)
    r1 = x - hi.astype(F32)
    mid = r1.astype(BF16)
    lo = (r1 - mid.astype(F32)).astype(BF16)
    c = (jnp.dot(tri, hi, preferred_element_type=F32) + jnp.dot(tri, mid, preferred_element_type=F32)
         + jnp.dot(tri, lo, preferred_element_type=F32)) + carry_ref[0:1]
    o_ref[...] = c
    carry_ref[...] = jnp.broadcast_to(c[t - 1:t], carry_ref.shape)


def cumsum_rows(x):
    s, w = x.shape
    t = _tile(s, 512)
    return pl.pallas_call(
        _cumsum_kernel,
        grid=(s // t,),
        in_specs=[pl.BlockSpec((t, w), lambda i: (i, 0))],
        out_specs=pl.BlockSpec((t, w), lambda i: (i, 0)),
        out_shape=jax.ShapeDtypeStruct((s, w), F32),
        scratch_shapes=[pltpu.VMEM((8, w), F32)],
        compiler_params=_params(("arbitrary",)),
        name="cumsum",
    )(x)


def _fox_kernel(qi_ref, kj_ref, q_ref, k_ref, v_ref, cq_ref, ck_ref, o_ref, m_ref, l_ref, acc_ref):
    p_id = pl.program_id(1)
    i, j = qi_ref[p_id], kj_ref[p_id]
    tq, tk = q_ref.shape[0], k_ref.shape[0]

    @pl.when(j == 0)
    def _():
        _init_softmax(m_ref, l_ref, acc_ref)

    s = lax.dot_general(q_ref[...], k_ref[...], (((1,), (1,)), ((), ())), preferred_element_type=F32)
    s = s + jnp.concatenate([cq_ref[...]] * (tk // LANES), axis=1) - ck_ref[...]
    rows = lax.broadcasted_iota(jnp.int32, s.shape, 0)
    cols = lax.broadcasted_iota(jnp.int32, s.shape, 1)
    s = jnp.where(j * tk + cols <= i * tq + rows, s, NEG_INF)
    _online_softmax_step(s, v_ref[...], m_ref, l_ref, acc_ref)

    @pl.when(j == (i * tq + tq - 1) // tk)
    def _():
        o_ref[...] = (acc_ref[...] * (1.0 / l_ref[...])).astype(o_ref.dtype)


def fox_attention(q, k, v, c):
    s = q.shape[0]
    t = _tile(s, 512)
    qi, kj = _causal_pairs(s // t, t, t)
    c_hs = c[:, :FOX_HEADS].T
    c_rep = jnp.broadcast_to(c_hs[:, :, None], (FOX_HEADS, s, LANES))
    c_row = c_hs.reshape(FOX_HEADS, 1, s)
    grid_spec = pltpu.PrefetchScalarGridSpec(
        num_scalar_prefetch=2,
        grid=(FOX_HEADS, qi.shape[0]),
        in_specs=[
            pl.BlockSpec((t, HEAD_DIM), lambda h, p, qi, kj: (qi[p], h)),
            pl.BlockSpec((t, HEAD_DIM), lambda h, p, qi, kj: (kj[p], h)),
            pl.BlockSpec((t, HEAD_DIM), lambda h, p, qi, kj: (kj[p], h)),
            pl.BlockSpec((None, t, LANES), lambda h, p, qi, kj: (h, qi[p], 0)),
            pl.BlockSpec((None, 1, t), lambda h, p, qi, kj: (h, 0, kj[p])),
        ],
        out_specs=pl.BlockSpec((t, HEAD_DIM), lambda h, p, qi, kj: (qi[p], h)),
        scratch_shapes=[pltpu.VMEM((t, 1), F32), pltpu.VMEM((t, 1), F32), pltpu.VMEM((t, HEAD_DIM), F32)],
    )
    return pl.pallas_call(
        _fox_kernel,
        grid_spec=grid_spec,
        out_shape=jax.ShapeDtypeStruct((s, FOX_WIDTH), BF16),
        compiler_params=_params(("parallel", "arbitrary")),
        name="fox_attention",
    )(qi, kj, q, k, v, c_rep, c_row)


def _pool_kernel(u_ref, halo_ref, w_ref, scale_ref, o_ref):
    i = pl.program_id(0)
    tm = u_ref.shape[0]
    u = u_ref[...]
    halo = jnp.where(i > 0, halo_ref[...], 0.0)
    x = jnp.concatenate([halo, u], axis=0)
    t1 = i * tm + lax.broadcasted_iota(jnp.int32, (tm, POOL_GROUP), 0) + 1
    for g, win in enumerate(POOL_WINDOWS):
        sl = slice(g * POOL_GROUP, (g + 1) * POOL_GROUP)
        acc = x[:, sl]
        span = 1
        while span < win:
            acc = acc + pltpu.roll(acc, span, 0)
            span *= 2
        cnt = jnp.minimum(t1, win).astype(F32)
        d = acc[POOL_HALO:] / cnt - u[:, sl]
        y = jnp.dot(d.astype(BF16), w_ref[g], preferred_element_type=F32)
        o_ref[:, sl] = (y * scale_ref[:, sl]).astype(o_ref.dtype)


def multiscale_pool(u, w_pool, scale):
    s, pw = u.shape
    tm = _tile(s, 512)
    hb = tm // POOL_HALO
    return pl.pallas_call(
        _pool_kernel,
        grid=(s // tm,),
        in_specs=[
            pl.BlockSpec((tm, pw), lambda i: (i, 0)),
            pl.BlockSpec((POOL_HALO, pw), lambda i: (jnp.maximum(i * hb - 1, 0), 0)),
            pl.BlockSpec(w_pool.shape, lambda i: (0, 0, 0)),
            pl.BlockSpec((1, pw), lambda i: (0, 0)),
        ],
        out_specs=pl.BlockSpec((tm, pw), lambda i: (i, 0)),
        out_shape=jax.ShapeDtypeStruct((s, pw), BF16),
        compiler_params=_params(("parallel",)),
        name="multiscale_pool",
    )(u, u, w_pool.astype(BF16), scale.reshape(1, pw))


def _swiglu_kernel(h_ref, w1_ref, w3_ref, o_ref):
    h = h_ref[...]
    a = jnp.dot(h, w1_ref[...], preferred_element_type=F32)
    b = jnp.dot(h, w3_ref[...], preferred_element_type=F32)
    o_ref[...] = (a * jax.nn.sigmoid(a) * b).astype(o_ref.dtype)


def swiglu_up(h, w1, w3):
    m, d = h.shape
    n = w1.shape[1]
    tm, tn = _tile(m, 1024), _tile(n, 512)
    wspec = pl.BlockSpec((d, tn), lambda i, j: (0, j))
    return pl.pallas_call(
        _swiglu_kernel,
        grid=(m // tm, n // tn),
        in_specs=[pl.BlockSpec((tm, d), lambda i, j: (i, 0)), wspec, wspec],
        out_specs=pl.BlockSpec((tm, tn), lambda i, j: (i, j)),
        out_shape=jax.ShapeDtypeStruct((m, n), BF16),
        compiler_params=_params(("parallel", "parallel")),
        name="swiglu_up",
    )(h, w1, w3)


def _resid_epilogue(acc, x_ref):
    return x_ref[...] + acc


def _gate_epilogue(acc, x_ref, p_ref, wp_ref):
    return x_ref[...] + jax.nn.sigmoid(acc) * jnp.dot(p_ref[...], wp_ref[...], preferred_element_type=F32)


def _logsig_epilogue(acc, b_ref):
    y = -(acc + b_ref[...])
    return -(jnp.maximum(y, 0.0) + jnp.log1p(jnp.exp(-jnp.abs(y))))


def _sigmoid_epilogue(acc):
    return jax.nn.sigmoid(acc)


def matmul_resid(a, b, x, *, tk=4096, tn=512, name):
    tm, tn = _tile(a.shape[0], 1024), _tile(b.shape[1], tn)
    return matmul(a, b, out_dtype=F32, epilogue=_resid_epilogue, extras=[_tile_extra(x, tm, tn)], tm=tm, tn=tn, tk=tk,
                  name=name)


def _pad_cols(w, mult):
    pad = -w.shape[1] % mult
    return jnp.pad(w, ((0, 0), (0, pad))) if pad else w


def ffn_and_embed(x, p_i, f_norm, f_w1, f_w3, f_w2, e_norm, e_w_gate, e_w_proj):
    h = rmsnorm_bf16(x, f_norm)
    hid_mult = 1024 if f_w1.shape[1] >= 1024 else LANES
    w1 = _pad_cols(f_w1, hid_mult).astype(BF16)
    w3 = _pad_cols(f_w3, hid_mult).astype(BF16)
    w2 = _pad_cols(f_w2.T, hid_mult).T.astype(BF16)
    u = swiglu_up(h, w1, w3)
    x = matmul_resid(u, w2, x, tk=2816, tn=1024, name="ffn_down")
    h = rmsnorm_bf16(x, e_norm)
    m, d = x.shape
    tm, tn = _tile(m, 1024), _tile(d, 512)
    pd = p_i.shape[1]
    extras = [_tile_extra(x, tm, tn), (p_i.astype(BF16), (tm, pd), lambda i, j, k: (i, 0)),
              (e_w_proj.astype(BF16), (pd, tn), lambda i, j, k: (0, j))]
    return matmul(h, e_w_gate.astype(BF16), out_dtype=F32, epilogue=_gate_epilogue, extras=extras, tm=tm, tn=tn,
                  name="embed_gate")


def even_mixer(x, rope, norm_g, w_in, q_norm, k_norm, cmp_pe, cmp_w1, cmp_w2, conv_w, w_out):
    s = x.shape[0]
    cw = conv_w.shape[1]
    h = rmsnorm_bf16(x, norm_g)
    o_kv = NSA_WIDTH
    o_gate = o_kv + 6 * KV_WIDTH
    o_conv = o_gate + 3 * NSA_HEADS
    wb = w_in.astype(BF16)
    kvw = lambda i: wb[:, o_kv + i * KV_WIDTH:o_kv + (i + 1) * KV_WIDTH]
    tile_gain = lambda g, n: jnp.tile(g, n // HEAD_DIM)

    q = head_proj(h, wb[:, :o_kv], gain=tile_gain(q_norm, NSA_WIDTH), rope=rope, scale=ATTN_SCALE, name="nsa_q_proj")
    k_cmp_in = head_proj(h, kvw(0), rope=rope, name="nsa_kcmp_proj")
    w_v = jnp.concatenate([kvw(1), kvw(3), kvw(5)], axis=1)
    v_all = matmul(h, w_v, out_dtype=BF16, name="nsa_v_proj")
    v_cmp_in, v_slc, v_win = (v_all[:, i * KV_WIDTH:(i + 1) * KV_WIDTH] for i in range(3))
    w_k = jnp.concatenate([kvw(2), kvw(4)], axis=1)
    g_k = jnp.concatenate([tile_gain(k_norm[1], KV_WIDTH), tile_gain(k_norm[2], KV_WIDTH)])
    k_both = head_proj(h, w_k, gain=g_k, rope=rope, name="nsa_k_proj")
    k_slc, k_win = k_both[:, :KV_WIDTH], k_both[:, KV_WIDTH:]
    w_g = _pad_cols(wb[:, o_gate:o_conv], LANES)
    gates = matmul(h, w_g, out_dtype=F32, epilogue=_sigmoid_epilogue, name="nsa_gate_proj")[:, :3 * NSA_HEADS]
    gates = gates.reshape(s, NSA_KV_HEADS, 3 * NSA_GROUP).transpose(1, 0, 2)
    z_conv = matmul(h, wb[:, o_conv:], out_dtype=F32, name="conv_proj")

    k_cmp = compress(k_cmp_in, cmp_pe[0], cmp_w1[0], cmp_w2[0], k_norm[0])
    v_cmp = compress(v_cmp_in, cmp_pe[1], cmp_w1[1], cmp_w2[1], None)
    o_cmp, sel = nsa_cmp_and_select(q, k_cmp, v_cmp)
    o_slc = nsa_selected(q, k_slc, v_slc, sel)
    o_nsa = nsa_window_combine(q, k_win, v_win, o_cmp, o_slc, gates)
    o_conv_out = short_conv(z_conv, conv_w)
    mixed = jnp.concatenate([o_nsa, o_conv_out], axis=1)
    return matmul_resid(mixed, w_out.astype(BF16), x, name="even_out_proj")


def odd_mixer(x, norm_g, w_in, f_bias, q_norm, k_norm, pool_w, pool_scale, w_out):
    h = rmsnorm_bf16(x, norm_g)
    o_fgate = 3 * FOX_WIDTH
    o_pool = o_fgate + FOX_HEADS
    wb = w_in.astype(BF16)
    tile_gain = lambda g: jnp.tile(g, FOX_WIDTH // HEAD_DIM)
    q = head_proj(h, wb[:, :FOX_WIDTH], gain=tile_gain(q_norm), scale=ATTN_SCALE, name="fox_q_proj")
    k = head_proj(h, wb[:, FOX_WIDTH:2 * FOX_WIDTH], gain=tile_gain(k_norm), name="fox_k_proj")
    v = matmul(h, wb[:, 2 * FOX_WIDTH:o_fgate], out_dtype=BF16, name="fox_v_proj")
    w_f = _pad_cols(wb[:, o_fgate:o_pool], LANES)
    b_f = jnp.pad(f_bias, (0, LANES - FOX_HEADS)).reshape(1, LANES)
    log_f = matmul(h, w_f, out_dtype=F32, epilogue=_logsig_epilogue, extras=[_row_extra(b_f, LANES)], name="fox_gate_proj")
    u = matmul(h, wb[:, o_pool:], out_dtype=F32, name="pool_proj")
    c = cumsum_rows(log_f)
    o_fox = fox_attention(q, k, v, c)
    o_pool_out = multiscale_pool(u, pool_w, pool_scale)
    mixed = jnp.concatenate([o_fox, o_pool_out], axis=1)
    return matmul_resid(mixed, w_out.astype(BF16), x, name="odd_out_proj")


def kernel(x, p, positions, a_norm, a_w_in, a_q_norm, a_k_norm, a_cmp_pe, a_cmp_w1, a_cmp_w2, a_conv_w, a_w_out, b_norm, b_w_in, b_f_bias, b_q_norm, b_k_norm, b_pool_w, b_pool_scale, b_w_out, f_norm, f_w1, f_w3, f_w2, e_norm, e_w_gate, e_w_proj):
    batch, s, d = x.shape
    depth = p.shape[0]
    outs = []
    for b in range(batch):
        xb = x[b]
        rope = rope_tables(positions[b])
        for i in range(depth):
            j = i // 2
            if i % 2 == 0:
                xb = even_mixer(xb, rope, a_norm[j], a_w_in[j], a_q_norm[j], a_k_norm[j], a_cmp_pe[j], a_cmp_w1[j],
                                a_cmp_w2[j], a_conv_w[j], a_w_out[j])
            else:
                xb = odd_mixer(xb, b_norm[j], b_w_in[j], b_f_bias[j], b_q_norm[j], b_k_norm[j], b_pool_w[j],
                               b_pool_scale[j], b_w_out[j])
            xb = ffn_and_embed(xb, p[i, b], f_norm[i], f_w1[i], f_w3[i], f_w2[i], e_norm[i], e_w_gate[i], e_w_proj[i])
        outs.append(xb)
    return jnp.stack(outs, axis=0)
```

```python
import functools

import jax
import jax.numpy as jnp
import numpy as np
from jax import lax
from jax.experimental import pallas as pl
from jax.experimental.pallas import tpu as pltpu

HEAD_DIM = 128
ROPE_DIM = HEAD_DIM // 4
ROPE_HALF = ROPE_DIM // 2
ROPE_THETA = 500000.0
NORM_EPS = 1e-6
ATTN_SCALE = HEAD_DIM ** -0.5
Q_BLOCK = 128

NSA_HEADS = 16
NSA_KV_HEADS = 4
NSA_GROUP = NSA_HEADS // NSA_KV_HEADS
NSA_WIDTH = NSA_HEADS * HEAD_DIM
KV_WIDTH = NSA_KV_HEADS * HEAD_DIM
CMP_LEN = 32
CMP_STRIDE = 16
SLC_LEN = 64
SLC_TOPK = 16
WIN_LEN = 512
CONV_K = 3

FOX_HEADS = 24
FOX_WIDTH = FOX_HEADS * HEAD_DIM
POOL_WINDOWS = (2, 4, 8, 16)
POOL_GROUP = 256
POOL_HALO = 16
CONV_HALO = 8

LANES = 128
VMEM_LIMIT_BYTES = 56 * 1024 * 1024
BF16 = jnp.bfloat16
F32 = jnp.float32
NEG_INF = float("-inf")
LOG2E = 1.4426950408889634
Q_SCALE = ATTN_SCALE * LOG2E
SEL_MASK = -1e30
KV_UNROLL = 4


def _params(sem):
    return pltpu.CompilerParams(dimension_semantics=sem, vmem_limit_bytes=VMEM_LIMIT_BYTES)


def _tile(dim, pref):
    if dim <= pref:
        return dim
    t = pref
    while dim % t:
        t //= 2
    return t


def _rmsnorm_kernel(x_ref, g_ref, o_ref):
    x = x_ref[...]
    ms = jnp.mean(x * x, axis=-1, keepdims=True)
    o_ref[...] = (x * lax.rsqrt(ms + NORM_EPS) * g_ref[...]).astype(o_ref.dtype)


def rmsnorm_bf16(x, g):
    s, d = x.shape
    tm = _tile(s, 256)
    return pl.pallas_call(
        _rmsnorm_kernel,
        grid=(s // tm,),
        in_specs=[pl.BlockSpec((tm, d), lambda i: (i, 0)), pl.BlockSpec((1, d), lambda i: (0, 0))],
        out_specs=pl.BlockSpec((tm, d), lambda i: (i, 0)),
        out_shape=jax.ShapeDtypeStruct((s, d), BF16),
        compiler_params=_params(("parallel",)),
        name="rmsnorm",
    )(x, g.reshape(1, d))


def _mm_kernel(*refs, nk, n_extra, epilogue):
    a_ref, b_ref = refs[0], refs[1]
    extra = refs[2:2 + n_extra]
    o_ref = refs[2 + n_extra]
    part = jnp.dot(a_ref[...], b_ref[...], preferred_element_type=F32)
    if nk == 1:
        o_ref[...] = epilogue(part, *extra).astype(o_ref.dtype)
        return
    acc_ref = refs[3 + n_extra]
    k = pl.program_id(2)

    @pl.when(k == 0)
    def _():
        acc_ref[...] = part

    @pl.when(k > 0)
    def _():
        acc_ref[...] += part

    @pl.when(k == nk - 1)
    def _():
        o_ref[...] = epilogue(acc_ref[...], *extra).astype(o_ref.dtype)


def matmul(a, b, *, out_dtype, epilogue=None, extras=(), tm=1024, tn=512, tk=4096, name="matmul"):
    m, kd = a.shape
    n = b.shape[1]
    tm, tn, tk = _tile(m, tm), _tile(n, tn), _tile(kd, tk)
    nk = kd // tk
    if epilogue is None:
        epilogue = lambda acc: acc
    in_specs = [pl.BlockSpec((tm, tk), lambda i, j, k: (i, k)), pl.BlockSpec((tk, tn), lambda i, j, k: (k, j))]
    in_specs += [pl.BlockSpec(bs, im) for _, bs, im in extras]
    scratch = [pltpu.VMEM((tm, tn), F32)] if nk > 1 else []
    return pl.pallas_call(
        functools.partial(_mm_kernel, nk=nk, n_extra=len(extras), epilogue=epilogue),
        grid=(m // tm, n // tn, nk),
        in_specs=in_specs,
        out_specs=pl.BlockSpec((tm, tn), lambda i, j, k: (i, j)),
        out_shape=jax.ShapeDtypeStruct((m, n), out_dtype),
        scratch_shapes=scratch,
        compiler_params=_params(("parallel", "parallel", "arbitrary")),
        name=name,
    )(a, b, *[e[0] for e in extras])


def _row_extra(arr, tn):
    return (arr, (1, tn), lambda i, j, k: (0, j))


def _tile_extra(arr, tm, tn):
    return (arr, (tm, tn), lambda i, j, k: (i, j))


def _rope_extras(rope, tm):
    return [(t, (tm, HEAD_DIM), lambda i, j, k: (i, 0)) for t in rope]


def _head_epilogue(acc, *refs, norm, rope, scale):
    refs = list(refs)
    g_ref = refs.pop(0) if norm else None
    if rope:
        cos_ref, sa_ref, sb_ref = refs
    outs = []
    for c in range(acc.shape[1] // HEAD_DIM):
        blk = acc[:, c * HEAD_DIM:(c + 1) * HEAD_DIM]
        if norm:
            ms = jnp.mean(blk * blk, axis=-1, keepdims=True)
            blk = blk * lax.rsqrt(ms + NORM_EPS) * g_ref[:, c * HEAD_DIM:(c + 1) * HEAD_DIM]
        if rope:
            blk = (blk * cos_ref[...] + pltpu.roll(blk, HEAD_DIM - ROPE_HALF, 1) * sa_ref[...]
                   + pltpu.roll(blk, ROPE_HALF, 1) * sb_ref[...])
        if scale != 1.0:
            blk = blk * scale
        outs.append(blk)
    return jnp.concatenate(outs, axis=1) if len(outs) > 1 else outs[0]


def head_proj(h, w, *, gain=None, rope=None, scale=1.0, name):
    m = h.shape[0]
    n = w.shape[1]
    tm, tn = _tile(m, 1024), _tile(n, 512)
    extras = []
    if gain is not None:
        extras.append(_row_extra(gain.reshape(1, n), tn))
    if rope is not None:
        extras += _rope_extras(rope, tm)
    ep = functools.partial(_head_epilogue, norm=gain is not None, rope=rope is not None, scale=scale)
    return matmul(h, w, out_dtype=BF16, epilogue=ep, extras=extras, tm=tm, tn=tn, name=name)


def _rope_table_kernel(pos_ref, freq_ref, cos_ref, sa_ref, sb_ref):
    ang = pos_ref[...] * freq_ref[...]
    lane = lax.broadcasted_iota(jnp.int32, ang.shape, 1)
    c, s = jnp.cos(ang), jnp.sin(ang)
    cos_ref[...] = jnp.where(lane < ROPE_DIM, c, 1.0)
    sa_ref[...] = jnp.where(lane < ROPE_HALF, -s, 0.0)
    sb_ref[...] = jnp.where((lane >= ROPE_HALF) & (lane < ROPE_DIM), s, 0.0)


def rope_tables(positions):
    s = positions.shape[0]
    inv_freq = ROPE_THETA ** (-jnp.arange(ROPE_HALF, dtype=F32) / ROPE_HALF)
    freq_row = jnp.zeros((HEAD_DIM,), F32).at[:ROPE_DIM].set(jnp.concatenate([inv_freq, inv_freq]))
    pos_rep = jnp.broadcast_to(positions.astype(F32)[:, None], (s, HEAD_DIM))
    tm = _tile(s, 1024)
    spec = pl.BlockSpec((tm, HEAD_DIM), lambda i: (i, 0))
    shp = jax.ShapeDtypeStruct((s, HEAD_DIM), F32)
    return pl.pallas_call(
        _rope_table_kernel,
        grid=(s // tm,),
        in_specs=[spec, pl.BlockSpec((1, HEAD_DIM), lambda i: (0, 0))],
        out_specs=[spec, spec, spec],
        out_shape=[shp, shp, shp],
        compiler_params=_params(("parallel",)),
        name="rope_tables",
    )(pos_rep, freq_row.reshape(1, HEAD_DIM))


def _gelu_tanh(x):
    return 0.5 * x * (1.0 + jnp.tanh(np.sqrt(2.0 / np.pi).astype(np.float32) * (x + 0.044715 * (x * x * x))))


def _compress_kernel(a_ref, w_ref, pe_ref, w2_ref, g_ref, o_ref, acc_ref, bias_ref, *, norm):
    l = pl.program_id(0)
    nl = pl.num_programs(0)
    a = a_ref[...]
    w = w_ref[0]
    pe_part = jnp.dot(pe_ref[0], w, preferred_element_type=F32)

    @pl.when(l == 0)
    def _():
        bias_ref[...] = pe_part
        for g in range(NSA_KV_HEADS):
            acc_ref[g] = jnp.dot(a[:, g * HEAD_DIM:(g + 1) * HEAD_DIM], w, preferred_element_type=F32)

    @pl.when(l > 0)
    def _():
        bias_ref[...] += pe_part
        for g in range(NSA_KV_HEADS):
            acc_ref[g] += jnp.dot(a[:, g * HEAD_DIM:(g + 1) * HEAD_DIM], w, preferred_element_type=F32)

    @pl.when(l == nl - 1)
    def _():
        n_chunk = a.shape[0]
        bias = bias_ref[0:1, :HEAD_DIM] + bias_ref[1:2, HEAD_DIM:]
        for g in range(NSA_KV_HEADS):
            p = acc_ref[g]
            hid = p[:, :HEAD_DIM] + pltpu.roll(p[:, HEAD_DIM:], n_chunk - 1, 0) + bias
            out = jnp.dot(_gelu_tanh(hid).astype(BF16), w2_ref[...], preferred_element_type=F32)
            if norm:
                ms = jnp.mean(out * out, axis=-1, keepdims=True)
                out = out * lax.rsqrt(ms + NORM_EPS) * g_ref[...]
            o_ref[:, g * HEAD_DIM:(g + 1) * HEAD_DIM] = out.astype(o_ref.dtype)


def compress(kv, pe, w1, w2, gain):
    s = kv.shape[0]
    n_chunk = s // CMP_STRIDE
    a = kv.reshape(n_chunk, CMP_STRIDE * KV_WIDTH)
    r = CMP_LEN // CMP_STRIDE
    w1r = w1.reshape(r, CMP_STRIDE, HEAD_DIM, HEAD_DIM)
    wcat = jnp.concatenate([w1r[0], w1r[1]], axis=-1).astype(BF16)
    pe_r = pe.reshape(r, CMP_STRIDE, HEAD_DIM).transpose(1, 0, 2)
    pe_l = jnp.zeros((CMP_STRIDE, 8, HEAD_DIM), F32).at[:, :r].set(pe_r).astype(BF16)
    norm = gain is not None
    g = (gain if norm else jnp.ones((HEAD_DIM,), F32)).reshape(1, HEAD_DIM)
    return pl.pallas_call(
        functools.partial(_compress_kernel, norm=norm),
        grid=(CMP_STRIDE,),
        in_specs=[
            pl.BlockSpec((n_chunk, KV_WIDTH), lambda l: (0, l)),
            pl.BlockSpec((1, HEAD_DIM, 2 * HEAD_DIM), lambda l: (l, 0, 0)),
            pl.BlockSpec((1, 8, HEAD_DIM), lambda l: (l, 0, 0)),
            pl.BlockSpec((HEAD_DIM, HEAD_DIM), lambda l: (0, 0)),
            pl.BlockSpec((1, HEAD_DIM), lambda l: (0, 0)),
        ],
        out_specs=pl.BlockSpec((n_chunk, KV_WIDTH), lambda l: (0, 0)),
        out_shape=jax.ShapeDtypeStruct((n_chunk, KV_WIDTH), BF16),
        scratch_shapes=[pltpu.VMEM((NSA_KV_HEADS, n_chunk, 2 * HEAD_DIM), F32), pltpu.VMEM((8, 2 * HEAD_DIM), F32)],
        compiler_params=_params(("arbitrary",)),
        name="nsa_compress",
    )(a, wcat, pe_l, w2.astype(BF16), g)


def _stack_heads(qb):
    return jnp.concatenate([qb[:, j * HEAD_DIM:(j + 1) * HEAD_DIM] for j in range(NSA_GROUP)], axis=0)


def _cmp_kernel(q_ref, k_ref, v_ref, ov_ref, o_ref, sel_ref):
    i = pl.program_id(1)
    tq = q_ref.shape[0]
    n_chunk = k_ref.shape[0]
    nb = sel_ref.shape[-1]
    q4 = _stack_heads(q_ref[...])
    s = lax.dot_general(q4, k_ref[...], (((1,), (1,)), ((), ())), preferred_element_type=F32)
    rows = lax.broadcasted_iota(jnp.int32, (NSA_GROUP * tq, n_chunk), 0)
    cols = lax.broadcasted_iota(jnp.int32, (NSA_GROUP * tq, n_chunk), 1)
    tpos = i * tq + rows % tq
    s = jnp.where(cols * CMP_STRIDE + (CMP_LEN - 1) <= tpos, s, NEG_INF)
    m = jnp.max(s, axis=-1, keepdims=True)
    m = jnp.where(m == NEG_INF, 0.0, m)
    e = jnp.exp2(s - m)
    p = e * (1.0 / jnp.maximum(jnp.sum(e, axis=-1, keepdims=True), 1e-30))
    o = jnp.dot(p.astype(BF16), v_ref[...], preferred_element_type=F32)
    for j in range(NSA_GROUP):
        o_ref[:, j * HEAD_DIM:(j + 1) * HEAD_DIM] = o[j * tq:(j + 1) * tq]

    psum = p[0:tq] + p[tq:2 * tq] + p[2 * tq:3 * tq] + p[3 * tq:4 * tq]
    p_hi = psum.astype(BF16)
    p_lo = (psum - p_hi.astype(F32)).astype(BF16)
    ov = ov_ref[...]
    imp = jnp.dot(p_hi, ov, preferred_element_type=F32) + jnp.dot(p_lo, ov, preferred_element_type=F32)

    blk = lax.broadcasted_iota(jnp.int32, (tq, nb), 1)
    t = i * tq + lax.broadcasted_iota(jnp.int32, (tq, nb), 0)
    cur = t // SLC_LEN
    forced = (blk == 0) | (blk == cur) | (blk == cur - 1)
    valid = blk * SLC_LEN <= t
    work = jnp.where(forced, jnp.inf, jnp.where(valid, imp, NEG_INF))
    sel = jnp.zeros((tq, nb), F32)
    blk_f = blk.astype(F32)
    for _ in range(min(SLC_TOPK, nb)):
        mx = jnp.max(work, axis=-1, keepdims=True)
        first = jnp.min(jnp.where(work == mx, blk_f, float(nb)), axis=-1, keepdims=True)
        pick = (blk_f == first) & (mx > NEG_INF)
        sel = jnp.where(pick, 1.0, sel)
        work = jnp.where(pick, NEG_INF, work)
    sel_ref[...] = jnp.where(sel > 0.5, 0.0, SEL_MASK).astype(sel_ref.dtype)


def nsa_cmp_and_select(q, k_cmp, v_cmp):
    s = q.shape[0]
    n_chunk = k_cmp.shape[0]
    nb = s // SLC_LEN
    tq = Q_BLOCK
    c_start = np.arange(n_chunk) * CMP_STRIDE
    s_start = np.arange(nb) * SLC_LEN
    overlap = np.clip(np.minimum(c_start[:, None] + CMP_LEN, s_start[None, :] + SLC_LEN)
                      - np.maximum(c_start[:, None], s_start[None, :]), 0, None).astype(np.float32) / CMP_LEN
    gw = NSA_GROUP * HEAD_DIM
    return pl.pallas_call(
        _cmp_kernel,
        grid=(NSA_KV_HEADS, s // tq),
        in_specs=[
            pl.BlockSpec((tq, gw), lambda g, i: (i, g)),
            pl.BlockSpec((n_chunk, HEAD_DIM), lambda g, i: (0, g)),
            pl.BlockSpec((n_chunk, HEAD_DIM), lambda g, i: (0, g)),
            pl.BlockSpec((n_chunk, nb), lambda g, i: (0, 0)),
        ],
        out_specs=[
            pl.BlockSpec((tq, gw), lambda g, i: (i, g)),
            pl.BlockSpec((None, tq, nb), lambda g, i: (g, i, 0)),
        ],
        out_shape=[jax.ShapeDtypeStruct((s, NSA_WIDTH), F32), jax.ShapeDtypeStruct((NSA_KV_HEADS, s, nb), BF16)],
        compiler_params=_params(("parallel", "parallel")),
        name="nsa_cmp_select",
    )(q, k_cmp, v_cmp, jnp.asarray(overlap, BF16))


def _flash_tile(q2, k2, v, m_ref, acc_ref, mask=None):
    tk = k2.shape[0]
    s = lax.dot_general(q2, k2, (((1,), (1,)), ((), ())), preferred_element_type=F32)
    if mask is not None:
        s = jnp.where(mask, s, NEG_INF)
    m_old = m_ref[...]
    m_new = jnp.maximum(m_old, jnp.max(s, axis=-1, keepdims=True))
    alpha = jnp.exp2(m_old - m_new)
    p = jnp.exp2(s - jnp.concatenate([m_new] * (tk // LANES), axis=1))
    v2 = jnp.concatenate([v, jnp.ones((tk, LANES), v.dtype)], axis=1)
    acc_ref[...] = jnp.concatenate([alpha, alpha], axis=1) * acc_ref[...] + jnp.dot(
        p.astype(BF16), v2, preferred_element_type=F32)
    m_ref[...] = m_new


def _init_flash(m_ref, acc_ref):
    m_ref[...] = jnp.full(m_ref.shape, NEG_INF, F32)
    acc_ref[...] = jnp.zeros(acc_ref.shape, F32)


def _flash_result(acc_ref):
    acc = acc_ref[...]
    return acc[:, :HEAD_DIM] * (1.0 / acc[:, HEAD_DIM:])


def _for_each_tile(n, tile_fn):
    def body(jj, carry):
        for u in range(KV_UNROLL):
            tile_fn(jj * KV_UNROLL + u)
        return carry

    lax.fori_loop(0, n // KV_UNROLL, body, 0)
    base = (n // KV_UNROLL) * KV_UNROLL
    g = KV_UNROLL // 2
    while g >= 1:
        take = (n & g) != 0

        @pl.when(take)
        def _(base=base, g=g):
            for u in range(g):
                tile_fn(base + u)

        base = base + jnp.where(take, g, 0)
        g //= 2


def _slc_kernel(q_ref, bias_ref, k_ref, kx_ref, v_ref, o_ref, q2_ref, m_ref, acc_ref, *, tk):
    i = pl.program_id(1)
    tq = q_ref.shape[0]
    cw = kx_ref.shape[1]
    _init_flash(m_ref, acc_ref)
    q4 = _stack_heads(q_ref[...])
    for c in range(q2_ref.shape[0]):
        bias = bias_ref[:, c * cw:(c + 1) * cw]
        q2_ref[c] = jnp.concatenate([q4, jnp.concatenate([bias] * NSA_GROUP, axis=0)], axis=1)

    def tile(j, masked):
        off = pl.multiple_of(j * tk, tk)
        k2 = jnp.concatenate([k_ref[pl.ds(off, tk), :], kx_ref[pl.ds(off, tk), :]], axis=1)
        q2 = q2_ref[(j * tk) // (SLC_LEN * cw)]
        mask = None
        if masked:
            rows = lax.broadcasted_iota(jnp.int32, (NSA_GROUP * tq, tk), 0)
            cols = lax.broadcasted_iota(jnp.int32, (NSA_GROUP * tq, tk), 1)
            mask = off + cols <= i * tq + rows % tq
        _flash_tile(q2, k2, v_ref[pl.ds(off, tk), :], m_ref, acc_ref, mask)

    last = (i * tq) // tk
    _for_each_tile(last, lambda j: tile(j, False))
    tile(last, True)
    o = _flash_result(acc_ref)
    for h in range(NSA_GROUP):
        o_ref[:, h * HEAD_DIM:(h + 1) * HEAD_DIM] = o[h * tq:(h + 1) * tq]


def nsa_selected(q, k_slc, v_slc, sel_bias):
    s = q.shape[0]
    nb = s // SLC_LEN
    cw = min(nb, LANES)
    tq, tk = Q_BLOCK, _tile(s, 512)
    key_blk = jnp.arange(s, dtype=jnp.int32)[:, None] // SLC_LEN
    key_onehot = (key_blk % cw == jnp.arange(cw, dtype=jnp.int32)[None, :]).astype(BF16)
    gw = NSA_GROUP * HEAD_DIM
    rows = NSA_GROUP * tq
    return pl.pallas_call(
        functools.partial(_slc_kernel, tk=tk),
        grid=(NSA_KV_HEADS, s // tq),
        in_specs=[
            pl.BlockSpec((tq, gw), lambda g, i: (i, g)),
            pl.BlockSpec((None, tq, nb), lambda g, i: (g, i, 0)),
            pl.BlockSpec((s, HEAD_DIM), lambda g, i: (0, g)),
            pl.BlockSpec((s, cw), lambda g, i: (0, 0)),
            pl.BlockSpec((s, HEAD_DIM), lambda g, i: (0, g)),
        ],
        out_specs=pl.BlockSpec((tq, gw), lambda g, i: (i, g)),
        out_shape=jax.ShapeDtypeStruct((s, NSA_WIDTH), F32),
        scratch_shapes=[pltpu.VMEM((nb // cw, rows, HEAD_DIM + cw), BF16), pltpu.VMEM((rows, LANES), F32),
                        pltpu.VMEM((rows, 2 * HEAD_DIM), F32)],
        compiler_params=_params(("parallel", "arbitrary")),
        name="nsa_selected",
    )(q, sel_bias, k_slc, key_onehot, v_slc)


WIN_TILES = WIN_LEN // Q_BLOCK + 1


def _win_kernel(*refs):
    q_ref = refs[0]
    k_refs = refs[1:1 + WIN_TILES]
    v_refs = refs[1 + WIN_TILES:1 + 2 * WIN_TILES]
    ocmp_ref, oslc_ref, gate_ref, o_ref = refs[1 + 2 * WIN_TILES:]
    i = pl.program_id(0)
    tq = q_ref.shape[0]
    q4 = _stack_heads(q_ref[...])
    kcat = jnp.concatenate([r[...] for r in k_refs], axis=0)
    vcat = jnp.concatenate([r[...] for r in v_refs], axis=0)
    s = lax.dot_general(q4, kcat, (((1,), (1,)), ((), ())), preferred_element_type=F32)
    rows = lax.broadcasted_iota(jnp.int32, s.shape, 0)
    cols = lax.broadcasted_iota(jnp.int32, s.shape, 1)
    tpos = i * tq + rows % tq
    wpos = (i - (WIN_TILES - 1)) * tq + cols
    ok = (wpos <= tpos) & (wpos > tpos - WIN_LEN) & (wpos >= 0)
    s = jnp.where(ok, s, NEG_INF)
    m = jnp.max(s, axis=-1, keepdims=True)
    e = jnp.exp2(s - m)
    p = e * (1.0 / jnp.sum(e, axis=-1, keepdims=True))
    o_win = jnp.dot(p.astype(BF16), vcat, preferred_element_type=F32)
    gates = gate_ref[...]
    for h in range(NSA_GROUP):
        sl = slice(h * HEAD_DIM, (h + 1) * HEAD_DIM)
        o = (gates[:, 3 * h:3 * h + 1] * ocmp_ref[:, sl] + gates[:, 3 * h + 1:3 * h + 2] * oslc_ref[:, sl]
             + gates[:, 3 * h + 2:3 * h + 3] * o_win[h * tq:(h + 1) * tq])
        o_ref[:, sl] = o.astype(o_ref.dtype)


def nsa_window_combine(q, k_win, v_win, o_cmp, o_slc, gates):
    s = q.shape[0]
    tq = Q_BLOCK
    gw = NSA_GROUP * HEAD_DIM
    back = WIN_TILES - 1
    kv_specs = [pl.BlockSpec((tq, HEAD_DIM), lambda i, g, d=d: (jnp.maximum(i - back + d, 0), g))
                for d in range(WIN_TILES)]
    blk = pl.BlockSpec((tq, gw), lambda i, g: (i, g))
    return pl.pallas_call(
        _win_kernel,
        grid=(s // tq, NSA_KV_HEADS),
        in_specs=[blk] + kv_specs + kv_specs + [blk, blk, pl.BlockSpec((None, tq, 3 * NSA_GROUP), lambda i, g: (g, i, 0))],
        out_specs=blk,
        out_shape=jax.ShapeDtypeStruct((s, NSA_WIDTH), BF16),
        compiler_params=_params(("parallel", "parallel")),
        name="nsa_window_combine",
    )(q, *([k_win] * WIN_TILES), *([v_win] * WIN_TILES), o_cmp, o_slc, gates)


def _conv_kernel(b_ref, c_ref, h_ref, ch_ref, hh_ref, w_ref, o_ref):
    i = pl.program_id(0)
    u = c_ref[...] * h_ref[...]
    halo = jnp.where(i > 0, ch_ref[...] * hh_ref[...], 0.0)
    x = jnp.concatenate([halo, u], axis=0)
    w = w_ref[...]
    y = (w[2:3] * x + w[1:2] * pltpu.roll(x, 1, 0) + w[0:1] * pltpu.roll(x, 2, 0))[CONV_HALO:]
    o_ref[...] = (b_ref[...] * y).astype(o_ref.dtype)


def short_conv(z, conv_w):
    s = z.shape[0]
    cw = conv_w.shape[1]
    tm, tc = _tile(s, 512), _tile(cw, 512)
    nc = cw // tc
    hb = tm // CONV_HALO
    halo = lambda off: pl.BlockSpec((CONV_HALO, tc), lambda i, j: (jnp.maximum(i * hb - 1, 0), off * nc + j))
    main = lambda off: pl.BlockSpec((tm, tc), lambda i, j: (i, off * nc + j))
    w8 = jnp.zeros((8, cw), F32).at[:CONV_K].set(conv_w)
    return pl.pallas_call(
        _conv_kernel,
        grid=(s // tm, nc),
        in_specs=[main(0), main(1), main(2), halo(1), halo(2), pl.BlockSpec((8, tc), lambda i, j: (0, j))],
        out_specs=pl.BlockSpec((tm, tc), lambda i, j: (i, j)),
        out_shape=jax.ShapeDtypeStruct((s, cw), BF16),
        compiler_params=_params(("parallel", "parallel")),
        name="short_conv",
    )(z, z, z, z, z, w8)


def _split3(x):
    hi = x.astype(BF16)
    r1 = x - hi.astype(F32)
    mid = r1.astype(BF16)
    lo = (r1 - mid.astype(F32)).astype(BF16)
    return hi, mid, lo


def _cumsum_kernel(x_ref, hi_ref, mid_ref, lo_ref, carry_ref):
    @pl.when(pl.program_id(0) == 0)
    def _():
        carry_ref[...] = jnp.zeros(carry_ref.shape, F32)

    x = x_ref[...]
    t = x.shape[0]
    tri = (lax.broadcasted_iota(jnp.int32, (t, t), 0) >= lax.broadcasted_iota(jnp.int32, (t, t), 1)).astype(BF16)
    c = sum(jnp.dot(tri, part, preferred_element_type=F32) for part in _split3(x)) + carry_ref[0:1]
    carry_ref[...] = jnp.broadcast_to(c[t - 1:t], carry_ref.shape)
    hi_ref[...], mid_ref[...], lo_ref[...] = _split3(c * LOG2E)


def cumsum_log2_split(x):
    s, w = x.shape
    t = _tile(s, 512)
    spec = pl.BlockSpec((t, w), lambda i: (i, 0))
    shp = jax.ShapeDtypeStruct((s, w), BF16)
    return pl.pallas_call(
        _cumsum_kernel,
        grid=(s // t,),
        in_specs=[spec],
        out_specs=[spec, spec, spec],
        out_shape=[shp, shp, shp],
        scratch_shapes=[pltpu.VMEM((8, w), F32)],
        compiler_params=_params(("arbitrary",)),
        name="cumsum",
    )(x)


def _fox_kernel(q_ref, qx_ref, k_ref, kx_ref, v_ref, o_ref, m_ref, acc_ref, *, t):
    i = pl.program_id(1)
    _init_flash(m_ref, acc_ref)
    q2 = jnp.concatenate([q_ref[...], qx_ref[...]], axis=1)

    def tile(j, masked):
        off = pl.multiple_of(j * t, t)
        k2 = jnp.concatenate([k_ref[pl.ds(off, t), :], kx_ref[pl.ds(off, t), :]], axis=1)
        mask = None
        if masked:
            mask = lax.broadcasted_iota(jnp.int32, (t, t), 1) <= lax.broadcasted_iota(jnp.int32, (t, t), 0)
        _flash_tile(q2, k2, v_ref[pl.ds(off, t), :], m_ref, acc_ref, mask)

    _for_each_tile(i, lambda j: tile(j, False))
    tile(i, True)
    o_ref[...] = _flash_result(acc_ref).astype(o_ref.dtype)


def fox_attention(q, k, v, c_parts):
    s = q.shape[0]
    t = _tile(s, 512)
    parts = jnp.stack([part[:, :FOX_HEADS] for part in c_parts], axis=-1)
    ones = jnp.ones_like(parts)
    widen = lambda a: jnp.pad(a, ((0, 0), (0, 0), (0, HEAD_DIM - a.shape[-1]))).reshape(s, FOX_WIDTH)
    qx = widen(jnp.concatenate([parts, ones], axis=-1))
    kx = widen(jnp.concatenate([ones, -parts], axis=-1))
    tile_spec = pl.BlockSpec((t, HEAD_DIM), lambda h, i: (i, h))
    full_spec = pl.BlockSpec((s, HEAD_DIM), lambda h, i: (0, h))
    return pl.pallas_call(
        functools.partial(_fox_kernel, t=t),
        grid=(FOX_HEADS, s // t),
        in_specs=[tile_spec, tile_spec, full_spec, full_spec, full_spec],
        out_specs=tile_spec,
        out_shape=jax.ShapeDtypeStruct((s, FOX_WIDTH), BF16),
        scratch_shapes=[pltpu.VMEM((t, LANES), F32), pltpu.VMEM((t, 2 * HEAD_DIM), F32)],
        compiler_params=_params(("parallel", "arbitrary")),
        name="fox_attention",
    )(q, qx, k, kx, v)


def _pool_kernel(u_ref, halo_ref, w_ref, scale_ref, o_ref):
    i = pl.program_id(0)
    tm = u_ref.shape[0]
    u = u_ref[...]
    halo = jnp.where(i > 0, halo_ref[...], 0.0)
    x = jnp.concatenate([halo, u], axis=0)
    t1 = i * tm + lax.broadcasted_iota(jnp.int32, (tm, POOL_GROUP), 0) + 1
    for g, win in enumerate(POOL_WINDOWS):
        sl = slice(g * POOL_GROUP, (g + 1) * POOL_GROUP)
        acc = x[:, sl]
        span = 1
        while span < win:
            acc = acc + pltpu.roll(acc, span, 0)
            span *= 2
        cnt = jnp.minimum(t1, win).astype(F32)
        d = acc[POOL_HALO:] / cnt - u[:, sl]
        y = jnp.dot(d.astype(BF16), w_ref[g], preferred_element_type=F32)
        o_ref[:, sl] = (y * scale_ref[:, sl]).astype(o_ref.dtype)


def multiscale_pool(u, w_pool, scale):
    s, pw = u.shape
    tm = _tile(s, 512)
    hb = tm // POOL_HALO
    return pl.pallas_call(
        _pool_kernel,
        grid=(s // tm,),
        in_specs=[
            pl.BlockSpec((tm, pw), lambda i: (i, 0)),
            pl.BlockSpec((POOL_HALO, pw), lambda i: (jnp.maximum(i * hb - 1, 0), 0)),
            pl.BlockSpec(w_pool.shape, lambda i: (0, 0, 0)),
            pl.BlockSpec((1, pw), lambda i: (0, 0)),
        ],
        out_specs=pl.BlockSpec((tm, pw), lambda i: (i, 0)),
        out_shape=jax.ShapeDtypeStruct((s, pw), BF16),
        compiler_params=_params(("parallel",)),
        name="multiscale_pool",
    )(u, u, w_pool.astype(BF16), scale.reshape(1, pw))


def _swiglu_kernel(h_ref, w1_ref, w3_ref, o_ref):
    h = h_ref[...]
    a = jnp.dot(h, w1_ref[...], preferred_element_type=F32)
    b = jnp.dot(h, w3_ref[...], preferred_element_type=F32)
    o_ref[...] = (a * jax.nn.sigmoid(a) * b).astype(o_ref.dtype)


def swiglu_up(h, w1, w3):
    m, d = h.shape
    n = w1.shape[1]
    tm, tn = _tile(m, 1024), _tile(n, 512)
    wspec = pl.BlockSpec((d, tn), lambda i, j: (0, j))
    return pl.pallas_call(
        _swiglu_kernel,
        grid=(m // tm, n // tn),
        in_specs=[pl.BlockSpec((tm, d), lambda i, j: (i, 0)), wspec, wspec],
        out_specs=pl.BlockSpec((tm, tn), lambda i, j: (i, j)),
        out_shape=jax.ShapeDtypeStruct((m, n), BF16),
        compiler_params=_params(("parallel", "parallel")),
        name="swiglu_up",
    )(h, w1, w3)


def _resid_epilogue(acc, x_ref):
    return x_ref[...] + acc


def _gate_epilogue(acc, x_ref, p_ref, wp_ref):
    return x_ref[...] + jax.nn.sigmoid(acc) * jnp.dot(p_ref[...], wp_ref[...], preferred_element_type=F32)


def _logsig_epilogue(acc, b_ref):
    y = -(acc + b_ref[...])
    return -(jnp.maximum(y, 0.0) + jnp.log1p(jnp.exp(-jnp.abs(y))))


def _sigmoid_epilogue(acc):
    return jax.nn.sigmoid(acc)


def matmul_resid(a, b, x, *, tk=4096, tn=512, name):
    tm, tn = _tile(a.shape[0], 1024), _tile(b.shape[1], tn)
    return matmul(a, b, out_dtype=F32, epilogue=_resid_epilogue, extras=[_tile_extra(x, tm, tn)], tm=tm, tn=tn, tk=tk,
                  name=name)


def _pad_cols(w, mult):
    pad = -w.shape[1] % mult
    return jnp.pad(w, ((0, 0), (0, pad))) if pad else w


def ffn_and_embed(x, p_i, f_norm, f_w1, f_w3, f_w2, e_norm, e_w_gate, e_w_proj):
    h = rmsnorm_bf16(x, f_norm)
    hid_mult = 1024 if f_w1.shape[1] >= 1024 else LANES
    w1 = _pad_cols(f_w1, hid_mult).astype(BF16)
    w3 = _pad_cols(f_w3, hid_mult).astype(BF16)
    w2 = _pad_cols(f_w2.T, hid_mult).T.astype(BF16)
    u = swiglu_up(h, w1, w3)
    x = matmul_resid(u, w2, x, tk=2816, tn=1024, name="ffn_down")
    h = rmsnorm_bf16(x, e_norm)
    m, d = x.shape
    tm, tn = _tile(m, 1024), _tile(d, 512)
    pd = p_i.shape[1]
    extras = [_tile_extra(x, tm, tn), (p_i.astype(BF16), (tm, pd), lambda i, j, k: (i, 0)),
              (e_w_proj.astype(BF16), (pd, tn), lambda i, j, k: (0, j))]
    return matmul(h, e_w_gate.astype(BF16), out_dtype=F32, epilogue=_gate_epilogue, extras=extras, tm=tm, tn=tn,
                  name="embed_gate")


def even_mixer(x, rope, norm_g, w_in, q_norm, k_norm, cmp_pe, cmp_w1, cmp_w2, conv_w, w_out):
    s = x.shape[0]
    cw = conv_w.shape[1]
    h = rmsnorm_bf16(x, norm_g)
    o_kv = NSA_WIDTH
    o_gate = o_kv + 6 * KV_WIDTH
    o_conv = o_gate + 3 * NSA_HEADS
    wb = w_in.astype(BF16)
    kvw = lambda i: wb[:, o_kv + i * KV_WIDTH:o_kv + (i + 1) * KV_WIDTH]
    tile_gain = lambda g, n: jnp.tile(g, n // HEAD_DIM)

    q = head_proj(h, wb[:, :o_kv], gain=tile_gain(q_norm, NSA_WIDTH), rope=rope, scale=Q_SCALE, name="nsa_q_proj")
    k_cmp_in = head_proj(h, kvw(0), rope=rope, name="nsa_kcmp_proj")
    w_v = jnp.concatenate([kvw(1), kvw(3), kvw(5)], axis=1)
    v_all = matmul(h, w_v, out_dtype=BF16, name="nsa_v_proj")
    v_cmp_in, v_slc, v_win = (v_all[:, i * KV_WIDTH:(i + 1) * KV_WIDTH] for i in range(3))
    w_k = jnp.concatenate([kvw(2), kvw(4)], axis=1)
    g_k = jnp.concatenate([tile_gain(k_norm[1], KV_WIDTH), tile_gain(k_norm[2], KV_WIDTH)])
    k_both = head_proj(h, w_k, gain=g_k, rope=rope, name="nsa_k_proj")
    k_slc, k_win = k_both[:, :KV_WIDTH], k_both[:, KV_WIDTH:]
    w_g = _pad_cols(wb[:, o_gate:o_conv], LANES)
    gates = matmul(h, w_g, out_dtype=F32, epilogue=_sigmoid_epilogue, name="nsa_gate_proj")[:, :3 * NSA_HEADS]
    gates = gates.reshape(s, NSA_KV_HEADS, 3 * NSA_GROUP).transpose(1, 0, 2)
    z_conv = matmul(h, wb[:, o_conv:], out_dtype=F32, name="conv_proj")

    k_cmp = compress(k_cmp_in, cmp_pe[0], cmp_w1[0], cmp_w2[0], k_norm[0])
    v_cmp = compress(v_cmp_in, cmp_pe[1], cmp_w1[1], cmp_w2[1], None)
    o_cmp, sel = nsa_cmp_and_select(q, k_cmp, v_cmp)
    o_slc = nsa_selected(q, k_slc, v_slc, sel)
    o_nsa = nsa_window_combine(q, k_win, v_win, o_cmp, o_slc, gates)
    o_conv_out = short_conv(z_conv, conv_w)
    mixed = jnp.concatenate([o_nsa, o_conv_out], axis=1)
    return matmul_resid(mixed, w_out.astype(BF16), x, name="even_out_proj")


def odd_mixer(x, norm_g, w_in, f_bias, q_norm, k_norm, pool_w, pool_scale, w_out):
    h = rmsnorm_bf16(x, norm_g)
    o_fgate = 3 * FOX_WIDTH
    o_pool = o_fgate + FOX_HEADS
    wb = w_in.astype(BF16)
    tile_gain = lambda g: jnp.tile(g, FOX_WIDTH // HEAD_DIM)
    q = head_proj(h, wb[:, :FOX_WIDTH], gain=tile_gain(q_norm), scale=Q_SCALE, name="fox_q_proj")
    k = head_proj(h, wb[:, FOX_WIDTH:2 * FOX_WIDTH], gain=tile_gain(k_norm), name="fox_k_proj")
    v = matmul(h, wb[:, 2 * FOX_WIDTH:o_fgate], out_dtype=BF16, name="fox_v_proj")
    w_f = _pad_cols(wb[:, o_fgate:o_pool], LANES)
    b_f = jnp.pad(f_bias, (0, LANES - FOX_HEADS)).reshape(1, LANES)
    log_f = matmul(h, w_f, out_dtype=F32, epilogue=_logsig_epilogue, extras=[_row_extra(b_f, LANES)], name="fox_gate_proj")
    u = matmul(h, wb[:, o_pool:], out_dtype=F32, name="pool_proj")
    o_fox = fox_attention(q, k, v, cumsum_log2_split(log_f))
    o_pool_out = multiscale_pool(u, pool_w, pool_scale)
    mixed = jnp.concatenate([o_fox, o_pool_out], axis=1)
    return matmul_resid(mixed, w_out.astype(BF16), x, name="odd_out_proj")


def kernel(x, p, positions, a_norm, a_w_in, a_q_norm, a_k_norm, a_cmp_pe, a_cmp_w1, a_cmp_w2, a_conv_w, a_w_out, b_norm, b_w_in, b_f_bias, b_q_norm, b_k_norm, b_pool_w, b_pool_scale, b_w_out, f_norm, f_w1, f_w3, f_w2, e_norm, e_w_gate, e_w_proj):
    batch, s, d = x.shape
    depth = p.shape[0]
    outs = []
    for b in range(batch):
        xb = x[b]
        rope = rope_tables(positions[b])
        for i in range(depth):
            j = i // 2
            if i % 2 == 0:
                xb = even_mixer(xb, rope, a_norm[j], a_w_in[j], a_q_norm[j], a_k_norm[j], a_cmp_pe[j], a_cmp_w1[j],
                                a_cmp_w2[j], a_conv_w[j], a_w_out[j])
            else:
                xb = odd_mixer(xb, b_norm[j], b_w_in[j], b_f_bias[j], b_q_norm[j], b_k_norm[j], b_pool_w[j],
                               b_pool_scale[j], b_w_out[j])
            xb = ffn_and_embed(xb, p[i, b], f_norm[i], f_w1[i], f_w3[i], f_w2[i], e_norm[i], e_w_gate[i], e_w_proj[i])
        outs.append(xb)
    return jnp.stack(outs, axis=0)
```

```python
import functools
from typing import NamedTuple, Optional

import jax
import jax.numpy as jnp
import numpy as np
from jax import lax
from jax.experimental import pallas as pl
from jax.experimental.pallas import tpu as pltpu

HEAD_DIM = 128
ROPE_DIM = HEAD_DIM // 4
ROPE_HALF = ROPE_DIM // 2
ROPE_THETA = 500000.0
NORM_EPS = 1e-6
ATTN_SCALE = HEAD_DIM ** -0.5
Q_BLOCK = 128

NSA_HEADS = 16
NSA_KV_HEADS = 4
NSA_GROUP = NSA_HEADS // NSA_KV_HEADS
NSA_WIDTH = NSA_HEADS * HEAD_DIM
KV_WIDTH = NSA_KV_HEADS * HEAD_DIM
CMP_LEN = 32
CMP_STRIDE = 16
SLC_LEN = 64
SLC_TOPK = 16
WIN_LEN = 512
CONV_K = 3

FOX_HEADS = 24
FOX_WIDTH = FOX_HEADS * HEAD_DIM
POOL_WINDOWS = (2, 4, 8, 16)
POOL_GROUP = 256
POOL_HALO = 16
CONV_HALO = 8

LANES = 128
VMEM_LIMIT_BYTES = 56 * 1024 * 1024
BF16 = jnp.bfloat16
F32 = jnp.float32
NEG_INF = float("-inf")
LOG2E = 1.4426950408889634
Q_SCALE = ATTN_SCALE * LOG2E
SEL_MASK = -1e30
KV_UNROLL = 8


def _params(sem):
    return pltpu.CompilerParams(dimension_semantics=sem, vmem_limit_bytes=VMEM_LIMIT_BYTES)


def _tile(dim, pref):
    if dim <= pref:
        return dim
    t = pref
    while dim % t:
        t //= 2
    return t


def _rmsnorm_kernel(x_ref, g_ref, o_ref):
    x = x_ref[...]
    ms = jnp.mean(x * x, axis=-1, keepdims=True)
    o_ref[...] = (x * lax.rsqrt(ms + NORM_EPS) * g_ref[...]).astype(o_ref.dtype)


def rmsnorm_bf16(x, g):
    s, d = x.shape
    tm = _tile(s, 256)
    return pl.pallas_call(
        _rmsnorm_kernel,
        grid=(s // tm,),
        in_specs=[pl.BlockSpec((tm, d), lambda i: (i, 0)), pl.BlockSpec((1, d), lambda i: (0, 0))],
        out_specs=pl.BlockSpec((tm, d), lambda i: (i, 0)),
        out_shape=jax.ShapeDtypeStruct((s, d), BF16),
        compiler_params=_params(("parallel",)),
        name="rmsnorm",
    )(x, g.reshape(1, d))


def _mm_kernel(*refs, nk, n_extra, epilogue):
    a_ref, b_ref = refs[0], refs[1]
    extra = refs[2:2 + n_extra]
    o_ref = refs[2 + n_extra]
    part = jnp.dot(a_ref[...], b_ref[...], preferred_element_type=F32)
    if nk == 1:
        o_ref[...] = epilogue(part, *extra).astype(o_ref.dtype)
        return
    acc_ref = refs[3 + n_extra]
    k = pl.program_id(2)

    @pl.when(k == 0)
    def _():
        acc_ref[...] = part

    @pl.when(k > 0)
    def _():
        acc_ref[...] += part

    @pl.when(k == nk - 1)
    def _():
        o_ref[...] = epilogue(acc_ref[...], *extra).astype(o_ref.dtype)


class Weight(NamedTuple):
    arr: jax.Array
    layer: Optional[int] = None
    col0: int = 0
    n: Optional[int] = None

    @property
    def k(self):
        return self.arr.shape[-2]

    @property
    def cols(self):
        return self.n if self.n is not None else self.arr.shape[-1] - self.col0

    def spec(self, tk, tn, k_of, j_of):
        assert self.col0 % tn == 0
        j0 = self.col0 // tn
        if self.layer is None:
            return pl.BlockSpec((tk, tn), lambda *g: (k_of(*g), j0 + j_of(*g)))
        return pl.BlockSpec((None, tk, tn), lambda *g: (self.layer, k_of(*g), j0 + j_of(*g)))


def matmul(a, b, *, out_dtype, epilogue=None, extras=(), tm=1024, tn=512, tk=4096, name="matmul"):
    if not isinstance(b, Weight):
        b = Weight(b)
    m, kd = a.shape
    n = b.cols
    tm, tn, tk = _tile(m, tm), _tile(n, tn), _tile(kd, tk)
    nk = kd // tk
    if epilogue is None:
        epilogue = lambda acc: acc
    in_specs = [pl.BlockSpec((tm, tk), lambda i, j, k: (i, k)), b.spec(tk, tn, lambda i, j, k: k, lambda i, j, k: j)]
    in_specs += [pl.BlockSpec(bs, im) for _, bs, im in extras]
    scratch = [pltpu.VMEM((tm, tn), F32)] if nk > 1 else []
    return pl.pallas_call(
        functools.partial(_mm_kernel, nk=nk, n_extra=len(extras), epilogue=epilogue),
        grid=(m // tm, n // tn, nk),
        in_specs=in_specs,
        out_specs=pl.BlockSpec((tm, tn), lambda i, j, k: (i, j)),
        out_shape=jax.ShapeDtypeStruct((m, n), out_dtype),
        scratch_shapes=scratch,
        compiler_params=_params(("parallel", "parallel", "arbitrary")),
        name=name,
    )(a, b.arr, *[e[0] for e in extras])


def _row_extra(arr, tn):
    return (arr, (1, tn), lambda i, j, k: (0, j))


def _tile_extra(arr, tm, tn):
    return (arr, (tm, tn), lambda i, j, k: (i, j))


def _rope_extras(rope, tm):
    return [(t, (tm, HEAD_DIM), lambda i, j, k: (i, 0)) for t in rope]


def _head_epilogue(acc, *refs, norm, rope, scale):
    refs = list(refs)
    g_ref = refs.pop(0) if norm else None
    if rope:
        cos_ref, sa_ref, sb_ref = refs
    outs = []
    for c in range(acc.shape[1] // HEAD_DIM):
        blk = acc[:, c * HEAD_DIM:(c + 1) * HEAD_DIM]
        if norm:
            ms = jnp.mean(blk * blk, axis=-1, keepdims=True)
            blk = blk * lax.rsqrt(ms + NORM_EPS) * g_ref[:, c * HEAD_DIM:(c + 1) * HEAD_DIM]
        if rope:
            blk = (blk * cos_ref[...] + pltpu.roll(blk, HEAD_DIM - ROPE_HALF, 1) * sa_ref[...]
                   + pltpu.roll(blk, ROPE_HALF, 1) * sb_ref[...])
        if scale != 1.0:
            blk = blk * scale
        outs.append(blk)
    return jnp.concatenate(outs, axis=1) if len(outs) > 1 else outs[0]


def head_proj(h, w, *, gain=None, rope=None, scale=1.0, name):
    m = h.shape[0]
    n = w.cols
    tm, tn = _tile(m, 1024), _tile(n, 512)
    extras = []
    if gain is not None:
        extras.append(_row_extra(gain.reshape(1, n), tn))
    if rope is not None:
        extras += _rope_extras(rope, tm)
    ep = functools.partial(_head_epilogue, norm=gain is not None, rope=rope is not None, scale=scale)
    return matmul(h, w, out_dtype=BF16, epilogue=ep, extras=extras, tm=tm, tn=tn, name=name)


def _rope_table_kernel(pos_ref, freq_ref, cos_ref, sa_ref, sb_ref):
    ang = pos_ref[...] * freq_ref[...]
    lane = lax.broadcasted_iota(jnp.int32, ang.shape, 1)
    c, s = jnp.cos(ang), jnp.sin(ang)
    cos_ref[...] = jnp.where(lane < ROPE_DIM, c, 1.0)
    sa_ref[...] = jnp.where(lane < ROPE_HALF, -s, 0.0)
    sb_ref[...] = jnp.where((lane >= ROPE_HALF) & (lane < ROPE_DIM), s, 0.0)


def rope_tables(positions):
    s = positions.shape[0]
    inv_freq = ROPE_THETA ** (-jnp.arange(ROPE_HALF, dtype=F32) / ROPE_HALF)
    freq_row = jnp.zeros((HEAD_DIM,), F32).at[:ROPE_DIM].set(jnp.concatenate([inv_freq, inv_freq]))
    pos_rep = jnp.broadcast_to(positions.astype(F32)[:, None], (s, HEAD_DIM))
    tm = _tile(s, 1024)
    spec = pl.BlockSpec((tm, HEAD_DIM), lambda i: (i, 0))
    shp = jax.ShapeDtypeStruct((s, HEAD_DIM), F32)
    return pl.pallas_call(
        _rope_table_kernel,
        grid=(s // tm,),
        in_specs=[spec, pl.BlockSpec((1, HEAD_DIM), lambda i: (0, 0))],
        out_specs=[spec, spec, spec],
        out_shape=[shp, shp, shp],
        compiler_params=_params(("parallel",)),
        name="rope_tables",
    )(pos_rep, freq_row.reshape(1, HEAD_DIM))


def _gelu_tanh(x):
    return 0.5 * x * (1.0 + jnp.tanh(np.sqrt(2.0 / np.pi).astype(np.float32) * (x + 0.044715 * (x * x * x))))


def _compress_kernel(a_ref, w_ref, pe_ref, w2_ref, g_ref, o_ref, acc_ref, bias_ref, *, norm):
    l = pl.program_id(0)
    nl = pl.num_programs(0)
    a = a_ref[...]
    w = w_ref[0]
    pe_part = jnp.dot(pe_ref[0], w, preferred_element_type=F32)

    @pl.when(l == 0)
    def _():
        bias_ref[...] = pe_part
        for g in range(NSA_KV_HEADS):
            acc_ref[g] = jnp.dot(a[:, g * HEAD_DIM:(g + 1) * HEAD_DIM], w, preferred_element_type=F32)

    @pl.when(l > 0)
    def _():
        bias_ref[...] += pe_part
        for g in range(NSA_KV_HEADS):
            acc_ref[g] += jnp.dot(a[:, g * HEAD_DIM:(g + 1) * HEAD_DIM], w, preferred_element_type=F32)

    @pl.when(l == nl - 1)
    def _():
        n_chunk = a.shape[0]
        bias = bias_ref[0:1, :HEAD_DIM] + bias_ref[1:2, HEAD_DIM:]
        for g in range(NSA_KV_HEADS):
            p = acc_ref[g]
            hid = p[:, :HEAD_DIM] + pltpu.roll(p[:, HEAD_DIM:], n_chunk - 1, 0) + bias
            out = jnp.dot(_gelu_tanh(hid).astype(BF16), w2_ref[...], preferred_element_type=F32)
            if norm:
                ms = jnp.mean(out * out, axis=-1, keepdims=True)
                out = out * lax.rsqrt(ms + NORM_EPS) * g_ref[...]
            o_ref[:, g * HEAD_DIM:(g + 1) * HEAD_DIM] = out.astype(o_ref.dtype)


def compress(kv, pe, w1, w2, gain):
    s = kv.shape[0]
    n_chunk = s // CMP_STRIDE
    a = kv.reshape(n_chunk, CMP_STRIDE * KV_WIDTH)
    r = CMP_LEN // CMP_STRIDE
    w1r = w1.reshape(r, CMP_STRIDE, HEAD_DIM, HEAD_DIM)
    wcat = jnp.concatenate([w1r[0], w1r[1]], axis=-1).astype(BF16)
    pe_r = pe.reshape(r, CMP_STRIDE, HEAD_DIM).transpose(1, 0, 2)
    pe_l = jnp.zeros((CMP_STRIDE, 8, HEAD_DIM), F32).at[:, :r].set(pe_r).astype(BF16)
    norm = gain is not None
    g = (gain if norm else jnp.ones((HEAD_DIM,), F32)).reshape(1, HEAD_DIM)
    return pl.pallas_call(
        functools.partial(_compress_kernel, norm=norm),
        grid=(CMP_STRIDE,),
        in_specs=[
            pl.BlockSpec((n_chunk, KV_WIDTH), lambda l: (0, l)),
            pl.BlockSpec((1, HEAD_DIM, 2 * HEAD_DIM), lambda l: (l, 0, 0)),
            pl.BlockSpec((1, 8, HEAD_DIM), lambda l: (l, 0, 0)),
            pl.BlockSpec((HEAD_DIM, HEAD_DIM), lambda l: (0, 0)),
            pl.BlockSpec((1, HEAD_DIM), lambda l: (0, 0)),
        ],
        out_specs=pl.BlockSpec((n_chunk, KV_WIDTH), lambda l: (0, 0)),
        out_shape=jax.ShapeDtypeStruct((n_chunk, KV_WIDTH), BF16),
        scratch_shapes=[pltpu.VMEM((NSA_KV_HEADS, n_chunk, 2 * HEAD_DIM), F32), pltpu.VMEM((8, 2 * HEAD_DIM), F32)],
        compiler_params=_params(("arbitrary",)),
        name="nsa_compress",
    )(a, wcat, pe_l, w2.astype(BF16), g)


def _stack_heads(qb):
    return jnp.concatenate([qb[:, j * HEAD_DIM:(j + 1) * HEAD_DIM] for j in range(NSA_GROUP)], axis=0)


def _cmp_kernel(q_ref, k_ref, v_ref, ov_ref, o_ref, sel_ref):
    i = pl.program_id(1)
    tq = q_ref.shape[0]
    n_chunk = k_ref.shape[0]
    nb = sel_ref.shape[-1]
    q4 = _stack_heads(q_ref[...])
    s = lax.dot_general(q4, k_ref[...], (((1,), (1,)), ((), ())), preferred_element_type=F32)
    rows = lax.broadcasted_iota(jnp.int32, (NSA_GROUP * tq, n_chunk), 0)
    cols = lax.broadcasted_iota(jnp.int32, (NSA_GROUP * tq, n_chunk), 1)
    tpos = i * tq + rows % tq
    s = jnp.where(cols * CMP_STRIDE + (CMP_LEN - 1) <= tpos, s, NEG_INF)
    m = jnp.max(s, axis=-1, keepdims=True)
    m = jnp.where(m == NEG_INF, 0.0, m)
    e = jnp.exp2(s - m)
    p = e * (1.0 / jnp.maximum(jnp.sum(e, axis=-1, keepdims=True), 1e-30))
    o = jnp.dot(p.astype(BF16), v_ref[...], preferred_element_type=F32)
    for j in range(NSA_GROUP):
        o_ref[:, j * HEAD_DIM:(j + 1) * HEAD_DIM] = o[j * tq:(j + 1) * tq]

    psum = p[0:tq] + p[tq:2 * tq] + p[2 * tq:3 * tq] + p[3 * tq:4 * tq]
    p_hi = psum.astype(BF16)
    p_lo = (psum - p_hi.astype(F32)).astype(BF16)
    ov = ov_ref[...]
    imp = jnp.dot(p_hi, ov, preferred_element_type=F32) + jnp.dot(p_lo, ov, preferred_element_type=F32)

    blk = lax.broadcasted_iota(jnp.int32, (tq, nb), 1)
    t = i * tq + lax.broadcasted_iota(jnp.int32, (tq, nb), 0)
    cur = t // SLC_LEN
    forced = (blk == 0) | (blk == cur) | (blk == cur - 1)
    valid = blk * SLC_LEN <= t
    work = jnp.where(forced, jnp.inf, jnp.where(valid, imp, NEG_INF))
    sel = jnp.zeros((tq, nb), F32)
    blk_f = blk.astype(F32)
    for _ in range(min(SLC_TOPK, nb)):
        mx = jnp.max(work, axis=-1, keepdims=True)
        first = jnp.min(jnp.where(work == mx, blk_f, float(nb)), axis=-1, keepdims=True)
        pick = (blk_f == first) & (mx > NEG_INF)
        sel = jnp.where(pick, 1.0, sel)
        work = jnp.where(pick, NEG_INF, work)
    sel_ref[...] = jnp.where(sel > 0.5, 0.0, SEL_MASK).astype(sel_ref.dtype)


def nsa_cmp_and_select(q, k_cmp, v_cmp):
    s = q.shape[0]
    n_chunk = k_cmp.shape[0]
    nb = s // SLC_LEN
    tq = Q_BLOCK
    c_start = np.arange(n_chunk) * CMP_STRIDE
    s_start = np.arange(nb) * SLC_LEN
    overlap = np.clip(np.minimum(c_start[:, None] + CMP_LEN, s_start[None, :] + SLC_LEN)
                      - np.maximum(c_start[:, None], s_start[None, :]), 0, None).astype(np.float32) / CMP_LEN
    gw = NSA_GROUP * HEAD_DIM
    return pl.pallas_call(
        _cmp_kernel,
        grid=(NSA_KV_HEADS, s // tq),
        in_specs=[
            pl.BlockSpec((tq, gw), lambda g, i: (i, g)),
            pl.BlockSpec((n_chunk, HEAD_DIM), lambda g, i: (0, g)),
            pl.BlockSpec((n_chunk, HEAD_DIM), lambda g, i: (0, g)),
            pl.BlockSpec((n_chunk, nb), lambda g, i: (0, 0)),
        ],
        out_specs=[
            pl.BlockSpec((tq, gw), lambda g, i: (i, g)),
            pl.BlockSpec((None, tq, nb), lambda g, i: (g, i, 0)),
        ],
        out_shape=[jax.ShapeDtypeStruct((s, NSA_WIDTH), F32), jax.ShapeDtypeStruct((NSA_KV_HEADS, s, nb), BF16)],
        compiler_params=_params(("parallel", "parallel")),
        name="nsa_cmp_select",
    )(q, k_cmp, v_cmp, jnp.asarray(overlap, BF16))


def _flash_tile(q2, k2, v, m_ref, acc_ref, mask=None):
    tk = k2.shape[0]
    s = lax.dot_general(q2, k2, (((1,), (1,)), ((), ())), preferred_element_type=F32)
    if mask is not None:
        s = jnp.where(mask, s, NEG_INF)
    m_old = m_ref[...]
    m_new = jnp.maximum(m_old, jnp.max(s, axis=-1, keepdims=True))
    alpha = jnp.exp2(m_old - m_new)
    p = jnp.exp2(s - jnp.concatenate([m_new] * (tk // LANES), axis=1))
    v2 = jnp.concatenate([v, jnp.ones((tk, LANES), v.dtype)], axis=1)
    acc_ref[...] = jnp.concatenate([alpha, alpha], axis=1) * acc_ref[...] + jnp.dot(
        p.astype(BF16), v2, preferred_element_type=F32)
    m_ref[...] = m_new


def _init_flash(m_ref, acc_ref):
    m_ref[...] = jnp.full(m_ref.shape, NEG_INF, F32)
    acc_ref[...] = jnp.zeros(acc_ref.shape, F32)


def _flash_result(acc_ref):
    acc = acc_ref[...]
    return acc[:, :HEAD_DIM] * (1.0 / acc[:, HEAD_DIM:])


def _for_each_tile(n, tile_fn):
    def body(jj, carry):
        for u in range(KV_UNROLL):
            tile_fn(jj * KV_UNROLL + u)
        return carry

    lax.fori_loop(0, n // KV_UNROLL, body, 0)
    base = (n // KV_UNROLL) * KV_UNROLL
    g = KV_UNROLL // 2
    while g >= 1:
        take = (n & g) != 0

        @pl.when(take)
        def _(base=base, g=g):
            for u in range(g):
                tile_fn(base + u)

        base = base + jnp.where(take, g, 0)
        g //= 2


def _slc_kernel(q_ref, bias_ref, k_ref, kx_ref, v_ref, o_ref, q2_ref, m_ref, acc_ref, *, tk):
    i = pl.program_id(1)
    tq = q_ref.shape[0]
    cw = kx_ref.shape[1]
    _init_flash(m_ref, acc_ref)
    q4 = _stack_heads(q_ref[...])
    for c in range(q2_ref.shape[0]):
        bias = bias_ref[:, c * cw:(c + 1) * cw]
        q2_ref[c] = jnp.concatenate([q4, jnp.concatenate([bias] * NSA_GROUP, axis=0)], axis=1)

    def tile(j, masked):
        off = pl.multiple_of(j * tk, tk)
        k2 = jnp.concatenate([k_ref[pl.ds(off, tk), :], kx_ref[pl.ds(off, tk), :]], axis=1)
        q2 = q2_ref[(j * tk) // (SLC_LEN * cw)]
        mask = None
        if masked:
            rows = lax.broadcasted_iota(jnp.int32, (NSA_GROUP * tq, tk), 0)
            cols = lax.broadcasted_iota(jnp.int32, (NSA_GROUP * tq, tk), 1)
            mask = off + cols <= i * tq + rows % tq
        _flash_tile(q2, k2, v_ref[pl.ds(off, tk), :], m_ref, acc_ref, mask)

    last = (i * tq) // tk
    _for_each_tile(last, lambda j: tile(j, False))
    tile(last, True)
    o = _flash_result(acc_ref)
    for h in range(NSA_GROUP):
        o_ref[:, h * HEAD_DIM:(h + 1) * HEAD_DIM] = o[h * tq:(h + 1) * tq]


def nsa_selected(q, k_slc, v_slc, sel_bias):
    s = q.shape[0]
    nb = s // SLC_LEN
    cw = min(nb, LANES)
    tq, tk = Q_BLOCK, _tile(s, 512)
    key_blk = jnp.arange(s, dtype=jnp.int32)[:, None] // SLC_LEN
    key_onehot = (key_blk % cw == jnp.arange(cw, dtype=jnp.int32)[None, :]).astype(BF16)
    gw = NSA_GROUP * HEAD_DIM
    rows = NSA_GROUP * tq
    return pl.pallas_call(
        functools.partial(_slc_kernel, tk=tk),
        grid=(NSA_KV_HEADS, s // tq),
        in_specs=[
            pl.BlockSpec((tq, gw), lambda g, i: (i, g)),
            pl.BlockSpec((None, tq, nb), lambda g, i: (g, i, 0)),
            pl.BlockSpec((s, HEAD_DIM), lambda g, i: (0, g)),
            pl.BlockSpec((s, cw), lambda g, i: (0, 0)),
            pl.BlockSpec((s, HEAD_DIM), lambda g, i: (0, g)),
        ],
        out_specs=pl.BlockSpec((tq, gw), lambda g, i: (i, g)),
        out_shape=jax.ShapeDtypeStruct((s, NSA_WIDTH), F32),
        scratch_shapes=[pltpu.VMEM((nb // cw, rows, HEAD_DIM + cw), BF16), pltpu.VMEM((rows, LANES), F32),
                        pltpu.VMEM((rows, 2 * HEAD_DIM), F32)],
        compiler_params=_params(("parallel", "arbitrary")),
        name="nsa_selected",
    )(q, sel_bias, k_slc, key_onehot, v_slc)


WIN_TILES = WIN_LEN // Q_BLOCK + 1


def _win_kernel(*refs):
    q_ref = refs[0]
    k_refs = refs[1:1 + WIN_TILES]
    v_refs = refs[1 + WIN_TILES:1 + 2 * WIN_TILES]
    ocmp_ref, oslc_ref, gate_ref, o_ref = refs[1 + 2 * WIN_TILES:]
    i = pl.program_id(0)
    tq = q_ref.shape[0]
    q4 = _stack_heads(q_ref[...])
    kcat = jnp.concatenate([r[...] for r in k_refs], axis=0)
    vcat = jnp.concatenate([r[...] for r in v_refs], axis=0)
    s = lax.dot_general(q4, kcat, (((1,), (1,)), ((), ())), preferred_element_type=F32)
    rows = lax.broadcasted_iota(jnp.int32, s.shape, 0)
    cols = lax.broadcasted_iota(jnp.int32, s.shape, 1)
    tpos = i * tq + rows % tq
    wpos = (i - (WIN_TILES - 1)) * tq + cols
    ok = (wpos <= tpos) & (wpos > tpos - WIN_LEN) & (wpos >= 0)
    s = jnp.where(ok, s, NEG_INF)
    m = jnp.max(s, axis=-1, keepdims=True)
    e = jnp.exp2(s - m)
    p = e * (1.0 / jnp.sum(e, axis=-1, keepdims=True))
    o_win = jnp.dot(p.astype(BF16), vcat, preferred_element_type=F32)
    gates = gate_ref[...]
    for h in range(NSA_GROUP):
        sl = slice(h * HEAD_DIM, (h + 1) * HEAD_DIM)
        o = (gates[:, 3 * h:3 * h + 1] * ocmp_ref[:, sl] + gates[:, 3 * h + 1:3 * h + 2] * oslc_ref[:, sl]
             + gates[:, 3 * h + 2:3 * h + 3] * o_win[h * tq:(h + 1) * tq])
        o_ref[:, sl] = o.astype(o_ref.dtype)


def nsa_window_combine(q, k_win, v_win, o_cmp, o_slc, gates, out_width):
    s = q.shape[0]
    tq = Q_BLOCK
    gw = NSA_GROUP * HEAD_DIM
    back = WIN_TILES - 1
    kv_specs = [pl.BlockSpec((tq, HEAD_DIM), lambda i, g, d=d: (jnp.maximum(i - back + d, 0), g))
                for d in range(WIN_TILES)]
    blk = pl.BlockSpec((tq, gw), lambda i, g: (i, g))
    return pl.pallas_call(
        _win_kernel,
        grid=(s // tq, NSA_KV_HEADS),
        in_specs=[blk] + kv_specs + kv_specs + [blk, blk, pl.BlockSpec((None, tq, 3 * NSA_GROUP), lambda i, g: (g, i, 0))],
        out_specs=blk,
        out_shape=jax.ShapeDtypeStruct((s, out_width), BF16),
        compiler_params=_params(("parallel", "parallel")),
        name="nsa_window_combine",
    )(q, *([k_win] * WIN_TILES), *([v_win] * WIN_TILES), o_cmp, o_slc, gates)


def _conv_kernel(b_ref, c_ref, h_ref, ch_ref, hh_ref, w_ref, dst_ref, o_ref):
    del dst_ref
    i = pl.program_id(0)
    u = c_ref[...] * h_ref[...]
    halo = jnp.where(i > 0, ch_ref[...] * hh_ref[...], 0.0)
    x = jnp.concatenate([halo, u], axis=0)
    w = w_ref[...]
    y = (w[2:3] * x + w[1:2] * pltpu.roll(x, 1, 0) + w[0:1] * pltpu.roll(x, 2, 0))[CONV_HALO:]
    o_ref[...] = (b_ref[...] * y).astype(o_ref.dtype)


def short_conv(z, conv_w, dst):
    s = z.shape[0]
    cw = conv_w.shape[1]
    tm, tc = _tile(s, 512), _tile(cw, 512)
    nc = cw // tc
    hb = tm // CONV_HALO
    col0 = dst.shape[1] - cw
    assert col0 % tc == 0
    halo = lambda off: pl.BlockSpec((CONV_HALO, tc), lambda i, j: (jnp.maximum(i * hb - 1, 0), off * nc + j))
    main = lambda off: pl.BlockSpec((tm, tc), lambda i, j: (i, off * nc + j))
    w8 = jnp.zeros((8, cw), F32).at[:CONV_K].set(conv_w)
    return pl.pallas_call(
        _conv_kernel,
        grid=(s // tm, nc),
        in_specs=[main(0), main(1), main(2), halo(1), halo(2), pl.BlockSpec((8, tc), lambda i, j: (0, j)),
                  pl.BlockSpec(memory_space=pl.ANY)],
        out_specs=pl.BlockSpec((tm, tc), lambda i, j: (i, col0 // tc + j)),
        out_shape=jax.ShapeDtypeStruct(dst.shape, dst.dtype),
        input_output_aliases={6: 0},
        compiler_params=_params(("parallel", "parallel")),
        name="short_conv",
    )(z, z, z, z, z, w8, dst)


def _split3(x):
    hi = x.astype(BF16)
    r1 = x - hi.astype(F32)
    mid = r1.astype(BF16)
    lo = (r1 - mid.astype(F32)).astype(BF16)
    return hi, mid, lo


def _cumsum_kernel(x_ref, hi_ref, mid_ref, lo_ref, carry_ref):
    @pl.when(pl.program_id(0) == 0)
    def _():
        carry_ref[...] = jnp.zeros(carry_ref.shape, F32)

    x = x_ref[...]
    t = x.shape[0]
    tri = (lax.broadcasted_iota(jnp.int32, (t, t), 0) >= lax.broadcasted_iota(jnp.int32, (t, t), 1)).astype(BF16)
    c = sum(jnp.dot(tri, part, preferred_element_type=F32) for part in _split3(x)) + carry_ref[0:1]
    carry_ref[...] = jnp.broadcast_to(c[t - 1:t], carry_ref.shape)
    hi_ref[...], mid_ref[...], lo_ref[...] = _split3(c * LOG2E)


def cumsum_log2_split(x):
    s, w = x.shape
    t = _tile(s, 512)
    spec = pl.BlockSpec((t, w), lambda i: (i, 0))
    shp = jax.ShapeDtypeStruct((s, w), BF16)
    return pl.pallas_call(
        _cumsum_kernel,
        grid=(s // t,),
        in_specs=[spec],
        out_specs=[spec, spec, spec],
        out_shape=[shp, shp, shp],
        scratch_shapes=[pltpu.VMEM((8, w), F32)],
        compiler_params=_params(("arbitrary",)),
        name="cumsum",
    )(x)


def _fox_kernel(q_ref, qx_ref, k_ref, kx_ref, v_ref, o_ref, m_ref, acc_ref, *, t):
    i = pl.program_id(1)
    _init_flash(m_ref, acc_ref)
    q2 = jnp.concatenate([q_ref[...], qx_ref[...]], axis=1)

    def tile(j, masked):
        off = pl.multiple_of(j * t, t)
        k2 = jnp.concatenate([k_ref[pl.ds(off, t), :], kx_ref[pl.ds(off, t), :]], axis=1)
        mask = None
        if masked:
            mask = lax.broadcasted_iota(jnp.int32, (t, t), 1) <= lax.broadcasted_iota(jnp.int32, (t, t), 0)
        _flash_tile(q2, k2, v_ref[pl.ds(off, t), :], m_ref, acc_ref, mask)

    _for_each_tile(i, lambda j: tile(j, False))
    tile(i, True)
    o_ref[...] = _flash_result(acc_ref).astype(o_ref.dtype)


def fox_attention(q, k, v, c_parts, out_width):
    s = q.shape[0]
    t = _tile(s, 512)
    parts = jnp.stack([part[:, :FOX_HEADS] for part in c_parts], axis=-1)
    ones = jnp.ones_like(parts)
    widen = lambda a: jnp.pad(a, ((0, 0), (0, 0), (0, HEAD_DIM - a.shape[-1]))).reshape(s, FOX_WIDTH)
    qx = widen(jnp.concatenate([parts, ones], axis=-1))
    kx = widen(jnp.concatenate([ones, -parts], axis=-1))
    tile_spec = pl.BlockSpec((t, HEAD_DIM), lambda h, i: (i, h))
    full_spec = pl.BlockSpec((s, HEAD_DIM), lambda h, i: (0, h))
    return pl.pallas_call(
        functools.partial(_fox_kernel, t=t),
        grid=(FOX_HEADS, s // t),
        in_specs=[tile_spec, tile_spec, full_spec, full_spec, full_spec],
        out_specs=tile_spec,
        out_shape=jax.ShapeDtypeStruct((s, out_width), BF16),
        scratch_shapes=[pltpu.VMEM((t, LANES), F32), pltpu.VMEM((t, 2 * HEAD_DIM), F32)],
        compiler_params=_params(("parallel", "arbitrary")),
        name="fox_attention",
    )(q, qx, k, kx, v)


def _pool_kernel(u_ref, halo_ref, w_ref, scale_ref, dst_ref, o_ref):
    del dst_ref
    i = pl.program_id(0)
    tm = u_ref.shape[0]
    u = u_ref[...]
    halo = jnp.where(i > 0, halo_ref[...], 0.0)
    x = jnp.concatenate([halo, u], axis=0)
    t1 = i * tm + lax.broadcasted_iota(jnp.int32, (tm, POOL_GROUP), 0) + 1
    for g, win in enumerate(POOL_WINDOWS):
        sl = slice(g * POOL_GROUP, (g + 1) * POOL_GROUP)
        acc = x[:, sl]
        span = 1
        while span < win:
            acc = acc + pltpu.roll(acc, span, 0)
            span *= 2
        cnt = jnp.minimum(t1, win).astype(F32)
        d = acc[POOL_HALO:] / cnt - u[:, sl]
        y = jnp.dot(d.astype(BF16), w_ref[g], preferred_element_type=F32)
        o_ref[:, sl] = (y * scale_ref[:, sl]).astype(o_ref.dtype)


def multiscale_pool(u, w_pool, scale, dst):
    s, pw = u.shape
    tm = _tile(s, 512)
    hb = tm // POOL_HALO
    col0 = dst.shape[1] - pw
    assert col0 % pw == 0
    return pl.pallas_call(
        _pool_kernel,
        grid=(s // tm,),
        in_specs=[
            pl.BlockSpec((tm, pw), lambda i: (i, 0)),
            pl.BlockSpec((POOL_HALO, pw), lambda i: (jnp.maximum(i * hb - 1, 0), 0)),
            pl.BlockSpec(w_pool.shape, lambda i: (0, 0, 0)),
            pl.BlockSpec((1, pw), lambda i: (0, 0)),
            pl.BlockSpec(memory_space=pl.ANY),
        ],
        out_specs=pl.BlockSpec((tm, pw), lambda i: (i, col0 // pw)),
        out_shape=jax.ShapeDtypeStruct(dst.shape, dst.dtype),
        input_output_aliases={4: 0},
        compiler_params=_params(("parallel",)),
        name="multiscale_pool",
    )(u, u, w_pool.astype(BF16), scale.reshape(1, pw), dst)


def _swiglu_kernel(h_ref, w1_ref, w3_ref, o_ref):
    h = h_ref[...]
    a = jnp.dot(h, w1_ref[...], preferred_element_type=F32)
    b = jnp.dot(h, w3_ref[...], preferred_element_type=F32)
    o_ref[...] = (a * jax.nn.sigmoid(a) * b).astype(o_ref.dtype)


def swiglu_up(h, w1, w3):
    m, d = h.shape
    n = w1.cols
    tm, tn = _tile(m, 1024), _tile(n, 512)
    specs = [w.spec(d, tn, lambda i, j: 0, lambda i, j: j) for w in (w1, w3)]
    return pl.pallas_call(
        _swiglu_kernel,
        grid=(m // tm, n // tn),
        in_specs=[pl.BlockSpec((tm, d), lambda i, j: (i, 0))] + specs,
        out_specs=pl.BlockSpec((tm, tn), lambda i, j: (i, j)),
        out_shape=jax.ShapeDtypeStruct((m, n), BF16),
        compiler_params=_params(("parallel", "parallel")),
        name="swiglu_up",
    )(h, w1.arr, w3.arr)


def _resid_epilogue(acc, x_ref):
    return x_ref[...] + acc


def _gate_epilogue(acc, x_ref, p_ref, wp_ref):
    return x_ref[...] + jax.nn.sigmoid(acc) * jnp.dot(p_ref[...], wp_ref[...], preferred_element_type=F32)


def _logsig_epilogue(acc, b_ref):
    y = -(acc + b_ref[...])
    return -(jnp.maximum(y, 0.0) + jnp.log1p(jnp.exp(-jnp.abs(y))))


def _sigmoid_epilogue(acc):
    return jax.nn.sigmoid(acc)


def matmul_resid(a, b, x, *, tk=4096, name):
    tm, tn = _tile(a.shape[0], 1024), _tile(b.cols, 512)
    return matmul(a, b, out_dtype=F32, epilogue=_resid_epilogue, extras=[_tile_extra(x, tm, tn)], tm=tm, tn=tn, tk=tk,
                  name=name)


def _k_tile(k, cap):
    for t in range(cap - cap % LANES, 0, -LANES):
        if k % t == 0:
            return t
    return k


FFN_DOWN_TK_CAP = 5632


def ffn_and_embed(x, p_i, layer, f_norm, w1, w3, w2, e_norm, w_gate, w_proj):
    h = rmsnorm_bf16(x, f_norm)
    u = swiglu_up(h, Weight(w1, layer), Weight(w3, layer))
    x = matmul_resid(u, Weight(w2, layer), x, tk=_k_tile(w2.shape[-2], FFN_DOWN_TK_CAP), name="ffn_down")
    h = rmsnorm_bf16(x, e_norm)
    m, d = x.shape
    tm, tn = _tile(m, 1024), _tile(d, 512)
    pd = p_i.shape[1]
    extras = [_tile_extra(x, tm, tn), (p_i.astype(BF16), (tm, pd), lambda i, j, k: (i, 0)),
              (w_proj, (None, pd, tn), lambda i, j, k: (layer, 0, j))]
    return matmul(h, Weight(w_gate, layer), out_dtype=F32, epilogue=_gate_epilogue, extras=extras, tm=tm, tn=tn,
                  name="embed_gate")


def even_mixer(x, rope, layer, norm_g, w_in, q_norm, k_norm, cmp_pe, cmp_w1, cmp_w2, conv_w, w_out):
    s = x.shape[0]
    cw = conv_w.shape[1]
    h = rmsnorm_bf16(x, norm_g)
    o_kv = NSA_WIDTH
    o_gate = o_kv + 6 * KV_WIDTH
    o_conv = o_gate + 3 * NSA_HEADS
    kv_cols = lambda i: slice(o_kv + i * KV_WIDTH, o_kv + (i + 1) * KV_WIDTH)
    tile_gain = lambda g, n: jnp.tile(g, n // HEAD_DIM)

    q = head_proj(h, Weight(w_in, layer, 0, NSA_WIDTH), gain=tile_gain(q_norm, NSA_WIDTH), rope=rope, scale=Q_SCALE,
                  name="nsa_q_proj")
    k_cmp_in = head_proj(h, Weight(w_in, layer, o_kv, KV_WIDTH), rope=rope, name="nsa_kcmp_proj")
    w_v = jnp.concatenate([w_in[layer, :, kv_cols(i)] for i in (1, 3, 5)], axis=1)
    v_all = matmul(h, w_v, out_dtype=BF16, name="nsa_v_proj")
    v_cmp_in, v_slc, v_win = (v_all[:, i * KV_WIDTH:(i + 1) * KV_WIDTH] for i in range(3))
    w_k = jnp.concatenate([w_in[layer, :, kv_cols(i)] for i in (2, 4)], axis=1)
    g_k = jnp.concatenate([tile_gain(k_norm[1], KV_WIDTH), tile_gain(k_norm[2], KV_WIDTH)])
    k_both = head_proj(h, Weight(w_k), gain=g_k, rope=rope, name="nsa_k_proj")
    k_slc, k_win = k_both[:, :KV_WIDTH], k_both[:, KV_WIDTH:]
    gates = matmul(h, Weight(w_in, layer, o_gate, LANES), out_dtype=F32, epilogue=_sigmoid_epilogue,
                   name="nsa_gate_proj")[:, :3 * NSA_HEADS]
    gates = gates.reshape(s, NSA_KV_HEADS, 3 * NSA_GROUP).transpose(1, 0, 2)
    z_conv = matmul(h, w_in[layer, :, o_conv:], out_dtype=F32, name="conv_proj")

    k_cmp = compress(k_cmp_in, cmp_pe[0], cmp_w1[0], cmp_w2[0], k_norm[0])
    v_cmp = compress(v_cmp_in, cmp_pe[1], cmp_w1[1], cmp_w2[1], None)
    o_cmp, sel = nsa_cmp_and_select(q, k_cmp, v_cmp)
    o_slc = nsa_selected(q, k_slc, v_slc, sel)
    mixed = nsa_window_combine(q, k_win, v_win, o_cmp, o_slc, gates, NSA_WIDTH + cw)
    mixed = short_conv(z_conv, conv_w, mixed)
    return matmul_resid(mixed, Weight(w_out, layer), x, name="even_out_proj")


def odd_mixer(x, layer, norm_g, w_in, f_bias, q_norm, k_norm, pool_w, pool_scale, w_out):
    h = rmsnorm_bf16(x, norm_g)
    o_fgate = 3 * FOX_WIDTH
    o_pool = o_fgate + FOX_HEADS
    pw = pool_scale.shape[0]
    tile_gain = lambda g: jnp.tile(g, FOX_WIDTH // HEAD_DIM)
    q = head_proj(h, Weight(w_in, layer, 0, FOX_WIDTH), gain=tile_gain(q_norm), scale=Q_SCALE, name="fox_q_proj")
    k = head_proj(h, Weight(w_in, layer, FOX_WIDTH, FOX_WIDTH), gain=tile_gain(k_norm), name="fox_k_proj")
    v = matmul(h, Weight(w_in, layer, 2 * FOX_WIDTH, FOX_WIDTH), out_dtype=BF16, name="fox_v_proj")
    b_f = jnp.pad(f_bias, (0, LANES - FOX_HEADS)).reshape(1, LANES)
    log_f = matmul(h, Weight(w_in, layer, o_fgate, LANES), out_dtype=F32, epilogue=_logsig_epilogue,
                   extras=[_row_extra(b_f, LANES)], name="fox_gate_proj")
    u = matmul(h, w_in[layer, :, o_pool:], out_dtype=F32, name="pool_proj")
    mixed = fox_attention(q, k, v, cumsum_log2_split(log_f), FOX_WIDTH + pw)
    mixed = multiscale_pool(u, pool_w, pool_scale, mixed)
    return matmul_resid(mixed, Weight(w_out, layer), x, name="odd_out_proj")


def kernel(x, p, positions, a_norm, a_w_in, a_q_norm, a_k_norm, a_cmp_pe, a_cmp_w1, a_cmp_w2, a_conv_w, a_w_out, b_norm, b_w_in, b_f_bias, b_q_norm, b_k_norm, b_pool_w, b_pool_scale, b_w_out, f_norm, f_w1, f_w3, f_w2, e_norm, e_w_gate, e_w_proj):
    batch, s, d = x.shape
    depth = p.shape[0]
    a_w_in, a_w_out, b_w_in, b_w_out, f_w1, f_w3, f_w2, e_w_gate, e_w_proj = (
        w.astype(BF16) for w in (a_w_in, a_w_out, b_w_in, b_w_out, f_w1, f_w3, f_w2, e_w_gate, e_w_proj))
    outs = []
    for b in range(batch):
        xb = x[b]
        rope = rope_tables(positions[b])
        for i in range(depth):
            j = i // 2
            if i % 2 == 0:
                xb = even_mixer(xb, rope, j, a_norm[j], a_w_in, a_q_norm[j], a_k_norm[j], a_cmp_pe[j], a_cmp_w1[j],
                                a_cmp_w2[j], a_conv_w[j], a_w_out)
            else:
                xb = odd_mixer(xb, j, b_norm[j], b_w_in, b_f_bias[j], b_q_norm[j], b_k_norm[j], b_pool_w[j],
                               b_pool_scale[j], b_w_out)
            xb = ffn_and_embed(xb, p[i, b], i, f_norm[i], f_w1, f_w3, f_w2, e_norm[i], e_w_gate, e_w_proj)
        outs.append(xb)
    return jnp.stack(outs, axis=0)
```

```python
import functools
from typing import NamedTuple, Optional

import jax
import jax.numpy as jnp
import numpy as np
from jax import lax
from jax.experimental import pallas as pl
from jax.experimental.pallas import tpu as pltpu

HEAD_DIM = 128
ROPE_DIM = HEAD_DIM // 4
ROPE_HALF = ROPE_DIM // 2
ROPE_THETA = 500000.0
NORM_EPS = 1e-6
ATTN_SCALE = HEAD_DIM ** -0.5
Q_BLOCK = 128

NSA_HEADS = 16
NSA_KV_HEADS = 4
NSA_GROUP = NSA_HEADS // NSA_KV_HEADS
NSA_WIDTH = NSA_HEADS * HEAD_DIM
KV_WIDTH = NSA_KV_HEADS * HEAD_DIM
CMP_LEN = 32
CMP_STRIDE = 16
SLC_LEN = 64
SLC_TOPK = 16
WIN_LEN = 512
CONV_K = 3

FOX_HEADS = 24
FOX_WIDTH = FOX_HEADS * HEAD_DIM
POOL_WINDOWS = (2, 4, 8, 16)
POOL_GROUP = 256
POOL_HALO = 16
CONV_HALO = 8

LANES = 128
MXU_WIDTH = 256
VMEM_LIMIT_BYTES = 56 * 1024 * 1024
BF16 = jnp.bfloat16
F32 = jnp.float32
NEG_INF = float("-inf")
LOG2E = 1.4426950408889634
Q_SCALE = ATTN_SCALE * LOG2E
SEL_MASK = -1e30
KV_UNROLL = 8
CMP_WIDTH_VARIANTS = 4
CMP_Q_TILE = 256


def _params(sem):
    return pltpu.CompilerParams(dimension_semantics=sem, vmem_limit_bytes=VMEM_LIMIT_BYTES)


def _tile(dim, pref):
    if dim <= pref:
        return dim
    t = pref
    while dim % t:
        t //= 2
    return t


def _rmsnorm_kernel(x_ref, g_ref, o_ref):
    x = x_ref[...]
    ms = jnp.mean(x * x, axis=-1, keepdims=True)
    o_ref[...] = (x * lax.rsqrt(ms + NORM_EPS) * g_ref[...]).astype(o_ref.dtype)


def rmsnorm_bf16(x, g):
    s, d = x.shape
    tm = _tile(s, 256)
    return pl.pallas_call(
        _rmsnorm_kernel,
        grid=(s // tm,),
        in_specs=[pl.BlockSpec((tm, d), lambda i: (i, 0)), pl.BlockSpec((1, d), lambda i: (0, 0))],
        out_specs=pl.BlockSpec((tm, d), lambda i: (i, 0)),
        out_shape=jax.ShapeDtypeStruct((s, d), BF16),
        compiler_params=_params(("parallel",)),
        name="rmsnorm",
    )(x, g.reshape(1, d))


def _mm_kernel(*refs, nk, n_extra, epilogue):
    a_ref, b_ref = refs[0], refs[1]
    extra = refs[2:2 + n_extra]
    o_ref = refs[2 + n_extra]
    part = jnp.dot(a_ref[...], b_ref[...], preferred_element_type=F32)
    if nk == 1:
        o_ref[...] = epilogue(part, *extra).astype(o_ref.dtype)
        return
    acc_ref = refs[3 + n_extra]
    k = pl.program_id(2)

    @pl.when(k == 0)
    def _():
        acc_ref[...] = part

    @pl.when(k > 0)
    def _():
        acc_ref[...] += part

    @pl.when(k == nk - 1)
    def _():
        o_ref[...] = epilogue(acc_ref[...], *extra).astype(o_ref.dtype)


class Weight(NamedTuple):
    arr: jax.Array
    layer: Optional[int] = None
    col0: int = 0
    n: Optional[int] = None

    @property
    def k(self):
        return self.arr.shape[-2]

    @property
    def cols(self):
        return self.n if self.n is not None else self.arr.shape[-1] - self.col0

    def spec(self, tk, tn, k_of, j_of):
        assert self.col0 % tn == 0
        j0 = self.col0 // tn
        if self.layer is None:
            return pl.BlockSpec((tk, tn), lambda *g: (k_of(*g), j0 + j_of(*g)))
        return pl.BlockSpec((None, tk, tn), lambda *g: (self.layer, k_of(*g), j0 + j_of(*g)))


def matmul(a, b, *, out_dtype, epilogue=None, extras=(), tm=1024, tn=512, tk=4096, name="matmul"):
    if not isinstance(b, Weight):
        b = Weight(b)
    m, kd = a.shape
    n = b.cols
    tm, tn, tk = _tile(m, tm), _tile(n, tn), _tile(kd, tk)
    nk = kd // tk
    if epilogue is None:
        epilogue = lambda acc: acc
    in_specs = [pl.BlockSpec((tm, tk), lambda i, j, k: (i, k)), b.spec(tk, tn, lambda i, j, k: k, lambda i, j, k: j)]
    in_specs += [pl.BlockSpec(bs, im) for _, bs, im in extras]
    scratch = [pltpu.VMEM((tm, tn), F32)] if nk > 1 else []
    return pl.pallas_call(
        functools.partial(_mm_kernel, nk=nk, n_extra=len(extras), epilogue=epilogue),
        grid=(m // tm, n // tn, nk),
        in_specs=in_specs,
        out_specs=pl.BlockSpec((tm, tn), lambda i, j, k: (i, j)),
        out_shape=jax.ShapeDtypeStruct((m, n), out_dtype),
        scratch_shapes=scratch,
        compiler_params=_params(("parallel", "parallel", "arbitrary")),
        name=name,
    )(a, b.arr, *[e[0] for e in extras])


def _row_extra(arr, tn):
    return (arr, (1, tn), lambda i, j, k: (0, j))


def _tile_extra(arr, tm, tn):
    return (arr, (tm, tn), lambda i, j, k: (i, j))


def _head_proj_kernel(*refs, norm, rope, scale):
    a_ref, b_ref, o_ref = refs[0], refs[1], refs[-1]
    extra = list(refs[2:-1])
    g_ref = extra.pop(0) if norm else None
    if rope:
        cos_ref, sa_ref, sb_ref = extra
    a = a_ref[...]
    tn = b_ref.shape[1]
    piece = min(MXU_WIDTH, tn)
    for p0 in range(0, tn, piece):
        acc = jnp.dot(a, b_ref[:, p0:p0 + piece], preferred_element_type=F32)
        for c0 in range(0, piece, HEAD_DIM):
            blk = acc[:, c0:c0 + HEAD_DIM]
            cols = slice(p0 + c0, p0 + c0 + HEAD_DIM)
            if norm:
                ms = jnp.mean(blk * blk, axis=-1, keepdims=True)
                blk = blk * lax.rsqrt(ms + NORM_EPS) * g_ref[:, cols]
            if rope:
                blk = (blk * cos_ref[...] + pltpu.roll(blk, HEAD_DIM - ROPE_HALF, 1) * sa_ref[...]
                       + pltpu.roll(blk, ROPE_HALF, 1) * sb_ref[...])
            if scale != 1.0:
                blk = blk * scale
            o_ref[:, cols] = blk.astype(o_ref.dtype)


def head_proj(h, w, *, gain=None, rope=None, scale=1.0, name):
    m, kd = h.shape
    n = w.cols
    tm, tn = _tile(m, 1024), _tile(n, 1024)
    arrays, specs = [], []
    if gain is not None:
        arrays.append(gain.reshape(1, n))
        specs.append(pl.BlockSpec((1, tn), lambda i, j: (0, j)))
    if rope is not None:
        arrays += list(rope)
        specs += [pl.BlockSpec((tm, HEAD_DIM), lambda i, j: (i, 0))] * len(rope)
    return pl.pallas_call(
        functools.partial(_head_proj_kernel, norm=gain is not None, rope=rope is not None, scale=scale),
        grid=(m // tm, n // tn),
        in_specs=[pl.BlockSpec((tm, kd), lambda i, j: (i, 0)), w.spec(kd, tn, lambda i, j: 0, lambda i, j: j)] + specs,
        out_specs=pl.BlockSpec((tm, tn), lambda i, j: (i, j)),
        out_shape=jax.ShapeDtypeStruct((m, n), BF16),
        compiler_params=_params(("parallel", "parallel")),
        name=name,
    )(h, w.arr, *arrays)


def _rope_table_kernel(pos_ref, freq_ref, cos_ref, sa_ref, sb_ref):
    ang = pos_ref[...] * freq_ref[...]
    lane = lax.broadcasted_iota(jnp.int32, ang.shape, 1)
    c, s = jnp.cos(ang), jnp.sin(ang)
    cos_ref[...] = jnp.where(lane < ROPE_DIM, c, 1.0)
    sa_ref[...] = jnp.where(lane < ROPE_HALF, -s, 0.0)
    sb_ref[...] = jnp.where((lane >= ROPE_HALF) & (lane < ROPE_DIM), s, 0.0)


def rope_tables(positions):
    s = positions.shape[0]
    inv_freq = ROPE_THETA ** (-jnp.arange(ROPE_HALF, dtype=F32) / ROPE_HALF)
    freq_row = jnp.zeros((HEAD_DIM,), F32).at[:ROPE_DIM].set(jnp.concatenate([inv_freq, inv_freq]))
    pos_rep = jnp.broadcast_to(positions.astype(F32)[:, None], (s, HEAD_DIM))
    tm = _tile(s, 1024)
    spec = pl.BlockSpec((tm, HEAD_DIM), lambda i: (i, 0))
    shp = jax.ShapeDtypeStruct((s, HEAD_DIM), F32)
    return pl.pallas_call(
        _rope_table_kernel,
        grid=(s // tm,),
        in_specs=[spec, pl.BlockSpec((1, HEAD_DIM), lambda i: (0, 0))],
        out_specs=[spec, spec, spec],
        out_shape=[shp, shp, shp],
        compiler_params=_params(("parallel",)),
        name="rope_tables",
    )(pos_rep, freq_row.reshape(1, HEAD_DIM))


def _gelu_tanh(x):
    return 0.5 * x * (1.0 + jnp.tanh(np.sqrt(2.0 / np.pi).astype(np.float32) * (x + 0.044715 * (x * x * x))))


def _compress_kernel(a_ref, w_ref, pe_ref, w2_ref, g_ref, o_ref, acc_ref, bias_ref, *, norm):
    l = pl.program_id(0)
    nl = pl.num_programs(0)
    a = a_ref[...]
    w = w_ref[0]
    pe_part = jnp.dot(pe_ref[0], w, preferred_element_type=F32)

    @pl.when(l == 0)
    def _():
        bias_ref[...] = pe_part
        for g in range(NSA_KV_HEADS):
            acc_ref[g] = jnp.dot(a[:, g * HEAD_DIM:(g + 1) * HEAD_DIM], w, preferred_element_type=F32)

    @pl.when(l > 0)
    def _():
        bias_ref[...] += pe_part
        for g in range(NSA_KV_HEADS):
            acc_ref[g] += jnp.dot(a[:, g * HEAD_DIM:(g + 1) * HEAD_DIM], w, preferred_element_type=F32)

    @pl.when(l == nl - 1)
    def _():
        n_chunk = a.shape[0]
        bias = bias_ref[0:1, :HEAD_DIM] + bias_ref[1:2, HEAD_DIM:]
        for g in range(NSA_KV_HEADS):
            p = acc_ref[g]
            hid = p[:, :HEAD_DIM] + pltpu.roll(p[:, HEAD_DIM:], n_chunk - 1, 0) + bias
            out = jnp.dot(_gelu_tanh(hid).astype(BF16), w2_ref[...], preferred_element_type=F32)
            if norm:
                ms = jnp.mean(out * out, axis=-1, keepdims=True)
                out = out * lax.rsqrt(ms + NORM_EPS) * g_ref[...]
            o_ref[:, g * HEAD_DIM:(g + 1) * HEAD_DIM] = out.astype(o_ref.dtype)


def compress(kv, pe, w1, w2, gain):
    s = kv.shape[0]
    n_chunk = s // CMP_STRIDE
    a = kv.reshape(n_chunk, CMP_STRIDE * KV_WIDTH)
    r = CMP_LEN // CMP_STRIDE
    w1r = w1.reshape(r, CMP_STRIDE, HEAD_DIM, HEAD_DIM)
    wcat = jnp.concatenate([w1r[0], w1r[1]], axis=-1).astype(BF16)
    pe_r = pe.reshape(r, CMP_STRIDE, HEAD_DIM).transpose(1, 0, 2)
    pe_l = jnp.zeros((CMP_STRIDE, 8, HEAD_DIM), F32).at[:, :r].set(pe_r).astype(BF16)
    norm = gain is not None
    g = (gain if norm else jnp.ones((HEAD_DIM,), F32)).reshape(1, HEAD_DIM)
    return pl.pallas_call(
        functools.partial(_compress_kernel, norm=norm),
        grid=(CMP_STRIDE,),
        in_specs=[
            pl.BlockSpec((n_chunk, KV_WIDTH), lambda l: (0, l)),
            pl.BlockSpec((1, HEAD_DIM, 2 * HEAD_DIM), lambda l: (l, 0, 0)),
            pl.BlockSpec((1, 8, HEAD_DIM), lambda l: (l, 0, 0)),
            pl.BlockSpec((HEAD_DIM, HEAD_DIM), lambda l: (0, 0)),
            pl.BlockSpec((1, HEAD_DIM), lambda l: (0, 0)),
        ],
        out_specs=pl.BlockSpec((n_chunk, KV_WIDTH), lambda l: (0, 0)),
        out_shape=jax.ShapeDtypeStruct((n_chunk, KV_WIDTH), BF16),
        scratch_shapes=[pltpu.VMEM((NSA_KV_HEADS, n_chunk, 2 * HEAD_DIM), F32), pltpu.VMEM((8, 2 * HEAD_DIM), F32)],
        compiler_params=_params(("arbitrary",)),
        name="nsa_compress",
    )(a, wcat, pe_l, w2.astype(BF16), g)


def _stack_heads(qb):
    return jnp.concatenate([qb[:, j * HEAD_DIM:(j + 1) * HEAD_DIM] for j in range(NSA_GROUP)], axis=0)


def _cmp_body(i, q_ref, k_ref, v_ref, ov_ref, o_ref, sel_ref, width, nbw):
    tq = q_ref.shape[0]
    nb = sel_ref.shape[-1]
    q4 = _stack_heads(q_ref[...])
    s = lax.dot_general(q4, k_ref[:width, :], (((1,), (1,)), ((), ())), preferred_element_type=F32)
    rows = lax.broadcasted_iota(jnp.int32, (NSA_GROUP * tq, width), 0)
    cols = lax.broadcasted_iota(jnp.int32, (NSA_GROUP * tq, width), 1)
    tpos = i * tq + rows % tq
    s = jnp.where(cols * CMP_STRIDE + (CMP_LEN - 1) <= tpos, s, NEG_INF)
    m = jnp.max(s, axis=-1, keepdims=True)
    m = jnp.where(m == NEG_INF, 0.0, m)
    e = jnp.exp2(s - m)
    p = e * (1.0 / jnp.maximum(jnp.sum(e, axis=-1, keepdims=True), 1e-30))
    o = jnp.dot(p.astype(BF16), v_ref[:width, :], preferred_element_type=F32)
    for j in range(NSA_GROUP):
        o_ref[:, j * HEAD_DIM:(j + 1) * HEAD_DIM] = o[j * tq:(j + 1) * tq]

    psum = p[0:tq] + p[tq:2 * tq] + p[2 * tq:3 * tq] + p[3 * tq:4 * tq]
    p_hi = psum.astype(BF16)
    p_lo = (psum - p_hi.astype(F32)).astype(BF16)
    ov = ov_ref[:width, :nbw]
    imp = jnp.dot(p_hi, ov, preferred_element_type=F32) + jnp.dot(p_lo, ov, preferred_element_type=F32)

    blk = lax.broadcasted_iota(jnp.int32, (tq, nbw), 1)
    t = i * tq + lax.broadcasted_iota(jnp.int32, (tq, nbw), 0)
    cur = t // SLC_LEN
    forced = (blk == 0) | (blk == cur) | (blk == cur - 1)
    valid = blk * SLC_LEN <= t
    work = jnp.where(forced, jnp.inf, jnp.where(valid, imp, NEG_INF))
    sel = jnp.zeros((tq, nbw), F32)
    n_valid = jnp.sum(jnp.where(work > NEG_INF, 1.0, 0.0), axis=-1, keepdims=True)
    for r in range(min(SLC_TOPK, nb)):
        first = jnp.argmax(work, axis=-1, keepdims=True)
        pick = (blk == first) & (n_valid > r)
        sel = jnp.where(pick, 1.0, sel)
        work = jnp.where(pick, NEG_INF, work)
    bias = jnp.where(sel > 0.5, 0.0, SEL_MASK)
    if nbw < nb:
        bias = jnp.concatenate([bias, jnp.full((tq, nb - nbw), SEL_MASK, F32)], axis=1)
    sel_ref[...] = bias.astype(sel_ref.dtype)


def _cmp_kernel(q_ref, k_ref, v_ref, ov_ref, o_ref, sel_ref, *, col_chunk):
    i = pl.program_id(1)
    tq = q_ref.shape[0]
    n_chunk = k_ref.shape[0]
    nb = sel_ref.shape[-1]
    n_var = n_chunk // col_chunk
    need = (i * tq + tq - CMP_LEN) // CMP_STRIDE + 1
    variant = jnp.minimum((need - 1) // col_chunk, n_var - 1)
    for v in range(n_var):
        width = (v + 1) * col_chunk
        i_max = (CMP_STRIDE * width + CMP_LEN - 1) // tq - 1
        blocks = ((i_max + 1) * tq - 1) // SLC_LEN + 1
        nbw = min(nb, -(-blocks // LANES) * LANES)

        @pl.when(variant == v)
        def _(width=width, nbw=nbw):
            _cmp_body(i, q_ref, k_ref, v_ref, ov_ref, o_ref, sel_ref, width, nbw)


def nsa_cmp_and_select(q, k_cmp, v_cmp):
    s = q.shape[0]
    n_chunk = k_cmp.shape[0]
    nb = s // SLC_LEN
    tq = _tile(s, CMP_Q_TILE)
    c_start = np.arange(n_chunk) * CMP_STRIDE
    s_start = np.arange(nb) * SLC_LEN
    overlap = np.clip(np.minimum(c_start[:, None] + CMP_LEN, s_start[None, :] + SLC_LEN)
                      - np.maximum(c_start[:, None], s_start[None, :]), 0, None).astype(np.float32) / CMP_LEN
    gw = NSA_GROUP * HEAD_DIM
    col_chunk = max(LANES, n_chunk // CMP_WIDTH_VARIANTS)
    return pl.pallas_call(
        functools.partial(_cmp_kernel, col_chunk=col_chunk),
        grid=(NSA_KV_HEADS, s // tq),
        in_specs=[
            pl.BlockSpec((tq, gw), lambda g, i: (i, g)),
            pl.BlockSpec((n_chunk, HEAD_DIM), lambda g, i: (0, g)),
            pl.BlockSpec((n_chunk, HEAD_DIM), lambda g, i: (0, g)),
            pl.BlockSpec((n_chunk, nb), lambda g, i: (0, 0)),
        ],
        out_specs=[
            pl.BlockSpec((tq, gw), lambda g, i: (i, g)),
            pl.BlockSpec((None, tq, nb), lambda g, i: (g, i, 0)),
        ],
        out_shape=[jax.ShapeDtypeStruct((s, NSA_WIDTH), F32), jax.ShapeDtypeStruct((NSA_KV_HEADS, s, nb), BF16)],
        compiler_params=_params(("parallel", "parallel")),
        name="nsa_cmp_select",
    )(q, k_cmp, v_cmp, jnp.asarray(overlap, BF16))


def _flash_tile(q2, k2, v, m_ref, acc_ref, mask=None):
    tk = k2.shape[0]
    s = lax.dot_general(q2, k2, (((1,), (1,)), ((), ())), preferred_element_type=F32)
    if mask is not None:
        s = jnp.where(mask, s, NEG_INF)
    m_old = m_ref[...]
    m_new = jnp.maximum(m_old, jnp.max(s, axis=-1, keepdims=True))
    alpha = jnp.exp2(m_old - m_new)
    p = jnp.exp2(s - jnp.concatenate([m_new] * (tk // LANES), axis=1))
    v2 = jnp.concatenate([v, jnp.ones((tk, LANES), v.dtype)], axis=1)
    acc_ref[...] = jnp.concatenate([alpha, alpha], axis=1) * acc_ref[...] + jnp.dot(
        p.astype(BF16), v2, preferred_element_type=F32)
    m_ref[...] = m_new


def _init_flash(m_ref, acc_ref):
    m_ref[...] = jnp.full(m_ref.shape, NEG_INF, F32)
    acc_ref[...] = jnp.zeros(acc_ref.shape, F32)


def _flash_result(acc_ref):
    acc = acc_ref[...]
    return acc[:, :HEAD_DIM] * (1.0 / acc[:, HEAD_DIM:])


def _for_each_tile(n, tile_fn):
    def body(jj, carry):
        for u in range(KV_UNROLL):
            tile_fn(jj * KV_UNROLL + u)
        return carry

    lax.fori_loop(0, n // KV_UNROLL, body, 0)
    base = (n // KV_UNROLL) * KV_UNROLL
    g = KV_UNROLL // 2
    while g >= 1:
        take = (n & g) != 0

        @pl.when(take)
        def _(base=base, g=g):
            for u in range(g):
                tile_fn(base + u)

        base = base + jnp.where(take, g, 0)
        g //= 2


def _slc_kernel(q_ref, bias_ref, k_ref, kx_ref, v_ref, o_ref, q2_ref, m_ref, acc_ref, *, tk):
    i = pl.program_id(1)
    tq = q_ref.shape[0]
    cw = kx_ref.shape[1]
    _init_flash(m_ref, acc_ref)
    q4 = _stack_heads(q_ref[...])
    for c in range(q2_ref.shape[0]):
        bias = bias_ref[:, c * cw:(c + 1) * cw]
        q2_ref[c] = jnp.concatenate([q4, jnp.concatenate([bias] * NSA_GROUP, axis=0)], axis=1)

    def tile(j, masked):
        off = pl.multiple_of(j * tk, tk)
        k2 = jnp.concatenate([k_ref[pl.ds(off, tk), :], kx_ref[pl.ds(off, tk), :]], axis=1)
        q2 = q2_ref[(j * tk) // (SLC_LEN * cw)]
        mask = None
        if masked:
            rows = lax.broadcasted_iota(jnp.int32, (NSA_GROUP * tq, tk), 0)
            cols = lax.broadcasted_iota(jnp.int32, (NSA_GROUP * tq, tk), 1)
            mask = off + cols <= i * tq + rows % tq
        _flash_tile(q2, k2, v_ref[pl.ds(off, tk), :], m_ref, acc_ref, mask)

    last = (i * tq) // tk
    _for_each_tile(last, lambda j: tile(j, False))
    tile(last, True)
    o = _flash_result(acc_ref)
    for h in range(NSA_GROUP):
        o_ref[:, h * HEAD_DIM:(h + 1) * HEAD_DIM] = o[h * tq:(h + 1) * tq]


def nsa_selected(q, k_slc, v_slc, sel_bias):
    s = q.shape[0]
    nb = s // SLC_LEN
    cw = min(nb, LANES)
    tq, tk = Q_BLOCK, _tile(s, 512)
    key_blk = jnp.arange(s, dtype=jnp.int32)[:, None] // SLC_LEN
    key_onehot = (key_blk % cw == jnp.arange(cw, dtype=jnp.int32)[None, :]).astype(BF16)
    gw = NSA_GROUP * HEAD_DIM
    rows = NSA_GROUP * tq
    return pl.pallas_call(
        functools.partial(_slc_kernel, tk=tk),
        grid=(NSA_KV_HEADS, s // tq),
        in_specs=[
            pl.BlockSpec((tq, gw), lambda g, i: (i, g)),
            pl.BlockSpec((None, tq, nb), lambda g, i: (g, i, 0)),
            pl.BlockSpec((s, HEAD_DIM), lambda g, i: (0, g)),
            pl.BlockSpec((s, cw), lambda g, i: (0, 0)),
            pl.BlockSpec((s, HEAD_DIM), lambda g, i: (0, g)),
        ],
        out_specs=pl.BlockSpec((tq, gw), lambda g, i: (i, g)),
        out_shape=jax.ShapeDtypeStruct((s, NSA_WIDTH), F32),
        scratch_shapes=[pltpu.VMEM((nb // cw, rows, HEAD_DIM + cw), BF16), pltpu.VMEM((rows, LANES), F32),
                        pltpu.VMEM((rows, 2 * HEAD_DIM), F32)],
        compiler_params=_params(("parallel", "arbitrary")),
        name="nsa_selected",
    )(q, sel_bias, k_slc, key_onehot, v_slc)


WIN_TILES = WIN_LEN // Q_BLOCK + 1


def _win_kernel(*refs):
    q_ref = refs[0]
    k_refs = refs[1:1 + WIN_TILES]
    v_refs = refs[1 + WIN_TILES:1 + 2 * WIN_TILES]
    ocmp_ref, oslc_ref, gate_ref, o_ref = refs[1 + 2 * WIN_TILES:]
    i = pl.program_id(0)
    tq = q_ref.shape[0]
    q4 = _stack_heads(q_ref[...])
    kcat = jnp.concatenate([r[...] for r in k_refs], axis=0)
    vcat = jnp.concatenate([r[...] for r in v_refs], axis=0)
    s = lax.dot_general(q4, kcat, (((1,), (1,)), ((), ())), preferred_element_type=F32)
    rows = lax.broadcasted_iota(jnp.int32, s.shape, 0)
    cols = lax.broadcasted_iota(jnp.int32, s.shape, 1)
    tpos = i * tq + rows % tq
    wpos = (i - (WIN_TILES - 1)) * tq + cols
    ok = (wpos <= tpos) & (wpos > tpos - WIN_LEN) & (wpos >= 0)
    s = jnp.where(ok, s, NEG_INF)
    m = jnp.max(s, axis=-1, keepdims=True)
    e = jnp.exp2(s - m)
    p = e * (1.0 / jnp.sum(e, axis=-1, keepdims=True))
    o_win = jnp.dot(p.astype(BF16), vcat, preferred_element_type=F32)
    gates = gate_ref[...]
    for h in range(NSA_GROUP):
        sl = slice(h * HEAD_DIM, (h + 1) * HEAD_DIM)
        o = (gates[:, 3 * h:3 * h + 1] * ocmp_ref[:, sl] + gates[:, 3 * h + 1:3 * h + 2] * oslc_ref[:, sl]
             + gates[:, 3 * h + 2:3 * h + 3] * o_win[h * tq:(h + 1) * tq])
        o_ref[:, sl] = o.astype(o_ref.dtype)


def nsa_window_combine(q, k_win, v_win, o_cmp, o_slc, gates, out_width):
    s = q.shape[0]
    tq = Q_BLOCK
    gw = NSA_GROUP * HEAD_DIM
    back = WIN_TILES - 1
    kv_specs = [pl.BlockSpec((tq, HEAD_DIM), lambda i, g, d=d: (jnp.maximum(i - back + d, 0), g))
                for d in range(WIN_TILES)]
    blk = pl.BlockSpec((tq, gw), lambda i, g: (i, g))
    return pl.pallas_call(
        _win_kernel,
        grid=(s // tq, NSA_KV_HEADS),
        in_specs=[blk] + kv_specs + kv_specs + [blk, blk, pl.BlockSpec((None, tq, 3 * NSA_GROUP), lambda i, g: (g, i, 0))],
        out_specs=blk,
        out_shape=jax.ShapeDtypeStruct((s, out_width), BF16),
        compiler_params=_params(("parallel", "parallel")),
        name="nsa_window_combine",
    )(q, *([k_win] * WIN_TILES), *([v_win] * WIN_TILES), o_cmp, o_slc, gates)


def _conv_kernel(b_ref, c_ref, h_ref, ch_ref, hh_ref, w_ref, dst_ref, o_ref):
    del dst_ref
    i = pl.program_id(0)
    u = c_ref[...] * h_ref[...]
    halo = jnp.where(i > 0, ch_ref[...] * hh_ref[...], 0.0)
    x = jnp.concatenate([halo, u], axis=0)
    w = w_ref[...]
    y = (w[2:3] * x + w[1:2] * pltpu.roll(x, 1, 0) + w[0:1] * pltpu.roll(x, 2, 0))[CONV_HALO:]
    o_ref[...] = (b_ref[...] * y).astype(o_ref.dtype)


def short_conv(z, conv_w, dst):
    s = z.shape[0]
    cw = conv_w.shape[1]
    tm, tc = _tile(s, 512), _tile(cw, 512)
    nc = cw // tc
    hb = tm // CONV_HALO
    col0 = dst.shape[1] - cw
    assert col0 % tc == 0
    halo = lambda off: pl.BlockSpec((CONV_HALO, tc), lambda i, j: (jnp.maximum(i * hb - 1, 0), off * nc + j))
    main = lambda off: pl.BlockSpec((tm, tc), lambda i, j: (i, off * nc + j))
    w8 = jnp.zeros((8, cw), F32).at[:CONV_K].set(conv_w)
    return pl.pallas_call(
        _conv_kernel,
        grid=(s // tm, nc),
        in_specs=[main(0), main(1), main(2), halo(1), halo(2), pl.BlockSpec((8, tc), lambda i, j: (0, j)),
                  pl.BlockSpec(memory_space=pl.ANY)],
        out_specs=pl.BlockSpec((tm, tc), lambda i, j: (i, col0 // tc + j)),
        out_shape=jax.ShapeDtypeStruct(dst.shape, dst.dtype),
        input_output_aliases={6: 0},
        compiler_params=_params(("parallel", "parallel")),
        name="short_conv",
    )(z, z, z, z, z, w8, dst)


def _split3(x):
    hi = x.astype(BF16)
    r1 = x - hi.astype(F32)
    mid = r1.astype(BF16)
    lo = (r1 - mid.astype(F32)).astype(BF16)
    return hi, mid, lo


def _cumsum_kernel(x_ref, hi_ref, mid_ref, lo_ref, carry_ref):
    @pl.when(pl.program_id(0) == 0)
    def _():
        carry_ref[...] = jnp.zeros(carry_ref.shape, F32)

    x = x_ref[...]
    t = x.shape[0]
    tri = (lax.broadcasted_iota(jnp.int32, (t, t), 0) >= lax.broadcasted_iota(jnp.int32, (t, t), 1)).astype(BF16)
    c = sum(jnp.dot(tri, part, preferred_element_type=F32) for part in _split3(x)) + carry_ref[0:1]
    carry_ref[...] = jnp.broadcast_to(c[t - 1:t], carry_ref.shape)
    hi_ref[...], mid_ref[...], lo_ref[...] = _split3(c * LOG2E)


def cumsum_log2_split(x):
    s, w = x.shape
    t = _tile(s, 512)
    spec = pl.BlockSpec((t, w), lambda i: (i, 0))
    shp = jax.ShapeDtypeStruct((s, w), BF16)
    return pl.pallas_call(
        _cumsum_kernel,
        grid=(s // t,),
        in_specs=[spec],
        out_specs=[spec, spec, spec],
        out_shape=[shp, shp, shp],
        scratch_shapes=[pltpu.VMEM((8, w), F32)],
        compiler_params=_params(("arbitrary",)),
        name="cumsum",
    )(x)


def _fox_kernel(q_ref, qx_ref, k_ref, kx_ref, v_ref, o_ref, m_ref, acc_ref, *, t):
    i = pl.program_id(1)
    _init_flash(m_ref, acc_ref)
    q2 = jnp.concatenate([q_ref[...], qx_ref[...]], axis=1)

    def tile(j, masked):
        off = pl.multiple_of(j * t, t)
        k2 = jnp.concatenate([k_ref[pl.ds(off, t), :], kx_ref[pl.ds(off, t), :]], axis=1)
        mask = None
        if masked:
            mask = lax.broadcasted_iota(jnp.int32, (t, t), 1) <= lax.broadcasted_iota(jnp.int32, (t, t), 0)
        _flash_tile(q2, k2, v_ref[pl.ds(off, t), :], m_ref, acc_ref, mask)

    _for_each_tile(i, lambda j: tile(j, False))
    tile(i, True)
    o_ref[...] = _flash_result(acc_ref).astype(o_ref.dtype)


def fox_attention(q, k, v, c_parts, out_width):
    s = q.shape[0]
    t = _tile(s, 512)
    parts = jnp.stack([part[:, :FOX_HEADS] for part in c_parts], axis=-1)
    ones = jnp.ones_like(parts)
    widen = lambda a: jnp.pad(a, ((0, 0), (0, 0), (0, HEAD_DIM - a.shape[-1]))).reshape(s, FOX_WIDTH)
    qx = widen(jnp.concatenate([parts, ones], axis=-1))
    kx = widen(jnp.concatenate([ones, -parts], axis=-1))
    tile_spec = pl.BlockSpec((t, HEAD_DIM), lambda h, i: (i, h))
    full_spec = pl.BlockSpec((s, HEAD_DIM), lambda h, i: (0, h))
    return pl.pallas_call(
        functools.partial(_fox_kernel, t=t),
        grid=(FOX_HEADS, s // t),
        in_specs=[tile_spec, tile_spec, full_spec, full_spec, full_spec],
        out_specs=tile_spec,
        out_shape=jax.ShapeDtypeStruct((s, out_width), BF16),
        scratch_shapes=[pltpu.VMEM((t, LANES), F32), pltpu.VMEM((t, 2 * HEAD_DIM), F32)],
        compiler_params=_params(("parallel", "arbitrary")),
        name="fox_attention",
    )(q, qx, k, kx, v)


def _pool_kernel(u_ref, halo_ref, w_ref, scale_ref, dst_ref, o_ref):
    del dst_ref
    i = pl.program_id(0)
    tm = u_ref.shape[0]
    u = u_ref[...]
    halo = jnp.where(i > 0, halo_ref[...], 0.0)
    x = jnp.concatenate([halo, u], axis=0)
    t1 = i * tm + lax.broadcasted_iota(jnp.int32, (tm, POOL_GROUP), 0) + 1
    for g, win in enumerate(POOL_WINDOWS):
        sl = slice(g * POOL_GROUP, (g + 1) * POOL_GROUP)
        acc = x[:, sl]
        span = 1
        while span < win:
            acc = acc + pltpu.roll(acc, span, 0)
            span *= 2
        cnt = jnp.minimum(t1, win).astype(F32)
        d = acc[POOL_HALO:] / cnt - u[:, sl]
        y = jnp.dot(d.astype(BF16), w_ref[g], preferred_element_type=F32)
        o_ref[:, sl] = (y * scale_ref[:, sl]).astype(o_ref.dtype)


def multiscale_pool(u, w_pool, scale, dst):
    s, pw = u.shape
    tm = _tile(s, 512)
    hb = tm // POOL_HALO
    col0 = dst.shape[1] - pw
    assert col0 % pw == 0
    return pl.pallas_call(
        _pool_kernel,
        grid=(s // tm,),
        in_specs=[
            pl.BlockSpec((tm, pw), lambda i: (i, 0)),
            pl.BlockSpec((POOL_HALO, pw), lambda i: (jnp.maximum(i * hb - 1, 0), 0)),
            pl.BlockSpec(w_pool.shape, lambda i: (0, 0, 0)),
            pl.BlockSpec((1, pw), lambda i: (0, 0)),
            pl.BlockSpec(memory_space=pl.ANY),
        ],
        out_specs=pl.BlockSpec((tm, pw), lambda i: (i, col0 // pw)),
        out_shape=jax.ShapeDtypeStruct(dst.shape, dst.dtype),
        input_output_aliases={4: 0},
        compiler_params=_params(("parallel",)),
        name="multiscale_pool",
    )(u, u, w_pool.astype(BF16), scale.reshape(1, pw), dst)


def _swiglu_kernel(h_ref, w1_ref, w3_ref, o_ref):
    h = h_ref[...]
    a = jnp.dot(h, w1_ref[...], preferred_element_type=F32)
    b = jnp.dot(h, w3_ref[...], preferred_element_type=F32)
    o_ref[...] = (a * jax.nn.sigmoid(a) * b).astype(o_ref.dtype)


def swiglu_up(h, w1, w3):
    m, d = h.shape
    n = w1.cols
    tm, tn = _tile(m, 2048), _tile(n, 512)
    specs = [w.spec(d, tn, lambda i, j: 0, lambda i, j: j) for w in (w1, w3)]
    return pl.pallas_call(
        _swiglu_kernel,
        grid=(m // tm, n // tn),
        in_specs=[pl.BlockSpec((tm, d), lambda i, j: (i, 0))] + specs,
        out_specs=pl.BlockSpec((tm, tn), lambda i, j: (i, j)),
        out_shape=jax.ShapeDtypeStruct((m, n), BF16),
        compiler_params=_params(("parallel", "parallel")),
        name="swiglu_up",
    )(h, w1.arr, w3.arr)


def _resid_epilogue(acc, x_ref):
    return x_ref[...] + acc


def _gate_epilogue(acc, x_ref, p_ref, wp_ref):
    return x_ref[...] + jax.nn.sigmoid(acc) * jnp.dot(p_ref[...], wp_ref[...], preferred_element_type=F32)


def _logsig_epilogue(acc, b_ref):
    y = -(acc + b_ref[...])
    return -(jnp.maximum(y, 0.0) + jnp.log1p(jnp.exp(-jnp.abs(y))))


def _sigmoid_epilogue(acc):
    return jax.nn.sigmoid(acc)


def matmul_resid(a, b, x, *, tk=4096, name):
    tm, tn = _tile(a.shape[0], 1024), _tile(b.cols, 512)
    return matmul(a, b, out_dtype=F32, epilogue=_resid_epilogue, extras=[_tile_extra(x, tm, tn)], tm=tm, tn=tn, tk=tk,
                  name=name)


def _k_tile(k, cap):
    for t in range(cap - cap % LANES, 0, -LANES):
        if k % t == 0:
            return t
    return k


FFN_DOWN_TK_CAP = 5632


def ffn_and_embed(x, p_i, layer, f_norm, w1, w3, w2, e_norm, w_gate, w_proj):
    h = rmsnorm_bf16(x, f_norm)
    u = swiglu_up(h, Weight(w1, layer), Weight(w3, layer))
    x = matmul_resid(u, Weight(w2, layer), x, tk=_k_tile(w2.shape[-2], FFN_DOWN_TK_CAP), name="ffn_down")
    h = rmsnorm_bf16(x, e_norm)
    m, d = x.shape
    tm, tn = _tile(m, 1024), _tile(d, 512)
    pd = p_i.shape[1]
    extras = [_tile_extra(x, tm, tn), (p_i.astype(BF16), (tm, pd), lambda i, j, k: (i, 0)),
              (w_proj, (None, pd, tn), lambda i, j, k: (layer, 0, j))]
    return matmul(h, Weight(w_gate, layer), out_dtype=F32, epilogue=_gate_epilogue, extras=extras, tm=tm, tn=tn,
                  name="embed_gate")


def even_mixer(x, rope, layer, norm_g, w_in, q_norm, k_norm, cmp_pe, cmp_w1, cmp_w2, conv_w, w_out):
    s = x.shape[0]
    cw = conv_w.shape[1]
    h = rmsnorm_bf16(x, norm_g)
    o_kv = NSA_WIDTH
    o_gate = o_kv + 6 * KV_WIDTH
    o_conv = o_gate + 3 * NSA_HEADS
    kv_cols = lambda i: slice(o_kv + i * KV_WIDTH, o_kv + (i + 1) * KV_WIDTH)
    tile_gain = lambda g, n: jnp.tile(g, n // HEAD_DIM)

    q = head_proj(h, Weight(w_in, layer, 0, NSA_WIDTH), gain=tile_gain(q_norm, NSA_WIDTH), rope=rope, scale=Q_SCALE,
                  name="nsa_q_proj")
    k_cmp_in = head_proj(h, Weight(w_in, layer, o_kv, KV_WIDTH), rope=rope, name="nsa_kcmp_proj")
    w_v = jnp.concatenate([w_in[layer, :, kv_cols(i)] for i in (1, 3, 5)], axis=1)
    v_all = matmul(h, w_v, out_dtype=BF16, name="nsa_v_proj")
    v_cmp_in, v_slc, v_win = (v_all[:, i * KV_WIDTH:(i + 1) * KV_WIDTH] for i in range(3))
    w_k = jnp.concatenate([w_in[layer, :, kv_cols(i)] for i in (2, 4)], axis=1)
    g_k = jnp.concatenate([tile_gain(k_norm[1], KV_WIDTH), tile_gain(k_norm[2], KV_WIDTH)])
    k_both = head_proj(h, Weight(w_k), gain=g_k, rope=rope, name="nsa_k_proj")
    k_slc, k_win = k_both[:, :KV_WIDTH], k_both[:, KV_WIDTH:]
    gates = matmul(h, Weight(w_in, layer, o_gate, LANES), out_dtype=F32, epilogue=_sigmoid_epilogue,
                   name="nsa_gate_proj")[:, :3 * NSA_HEADS]
    gates = gates.reshape(s, NSA_KV_HEADS, 3 * NSA_GROUP).transpose(1, 0, 2)
    z_conv = matmul(h, w_in[layer, :, o_conv:], out_dtype=F32, name="conv_proj")

    k_cmp = compress(k_cmp_in, cmp_pe[0], cmp_w1[0], cmp_w2[0], k_norm[0])
    v_cmp = compress(v_cmp_in, cmp_pe[1], cmp_w1[1], cmp_w2[1], None)
    o_cmp, sel = nsa_cmp_and_select(q, k_cmp, v_cmp)
    o_slc = nsa_selected(q, k_slc, v_slc, sel)
    mixed = nsa_window_combine(q, k_win, v_win, o_cmp, o_slc, gates, NSA_WIDTH + cw)
    mixed = short_conv(z_conv, conv_w, mixed)
    return matmul_resid(mixed, Weight(w_out, layer), x, name="even_out_proj")


def odd_mixer(x, layer, norm_g, w_in, f_bias, q_norm, k_norm, pool_w, pool_scale, w_out):
    h = rmsnorm_bf16(x, norm_g)
    o_fgate = 3 * FOX_WIDTH
    o_pool = o_fgate + FOX_HEADS
    pw = pool_scale.shape[0]
    tile_gain = lambda g: jnp.tile(g, FOX_WIDTH // HEAD_DIM)
    q = head_proj(h, Weight(w_in, layer, 0, FOX_WIDTH), gain=tile_gain(q_norm), scale=Q_SCALE, name="fox_q_proj")
    k = head_proj(h, Weight(w_in, layer, FOX_WIDTH, FOX_WIDTH), gain=tile_gain(k_norm), name="fox_k_proj")
    v = matmul(h, Weight(w_in, layer, 2 * FOX_WIDTH, FOX_WIDTH), out_dtype=BF16, name="fox_v_proj")
    b_f = jnp.pad(f_bias, (0, LANES - FOX_HEADS)).reshape(1, LANES)
    log_f = matmul(h, Weight(w_in, layer, o_fgate, LANES), out_dtype=F32, epilogue=_logsig_epilogue,
                   extras=[_row_extra(b_f, LANES)], name="fox_gate_proj")
    u = matmul(h, w_in[layer, :, o_pool:], out_dtype=F32, name="pool_proj")
    mixed = fox_attention(q, k, v, cumsum_log2_split(log_f), FOX_WIDTH + pw)
    mixed = multiscale_pool(u, pool_w, pool_scale, mixed)
    return matmul_resid(mixed, Weight(w_out, layer), x, name="odd_out_proj")


def kernel(x, p, positions, a_norm, a_w_in, a_q_norm, a_k_norm, a_cmp_pe, a_cmp_w1, a_cmp_w2, a_conv_w, a_w_out, b_norm, b_w_in, b_f_bias, b_q_norm, b_k_norm, b_pool_w, b_pool_scale, b_w_out, f_norm, f_w1, f_w3, f_w2, e_norm, e_w_gate, e_w_proj):
    batch, s, d = x.shape
    depth = p.shape[0]
    a_w_in, a_w_out, b_w_in, b_w_out, f_w1, f_w3, f_w2, e_w_gate, e_w_proj = (
        w.astype(BF16) for w in (a_w_in, a_w_out, b_w_in, b_w_out, f_w1, f_w3, f_w2, e_w_gate, e_w_proj))
    outs = []
    for b in range(batch):
        xb = x[b]
        rope = rope_tables(positions[b])
        for i in range(depth):
            j = i // 2
            if i % 2 == 0:
                xb = even_mixer(xb, rope, j, a_norm[j], a_w_in, a_q_norm[j], a_k_norm[j], a_cmp_pe[j], a_cmp_w1[j],
                                a_cmp_w2[j], a_conv_w[j], a_w_out)
            else:
                xb = odd_mixer(xb, j, b_norm[j], b_w_in, b_f_bias[j], b_q_norm[j], b_k_norm[j], b_pool_w[j],
                               b_pool_scale[j], b_w_out)
            xb = ffn_and_embed(xb, p[i, b], i, f_norm[i], f_w1, f_w3, f_w2, e_norm[i], e_w_gate, e_w_proj)
        outs.append(xb)
    return jnp.stack(outs, axis=0)
```

```python
import functools
from typing import NamedTuple, Optional

import jax
import jax.numpy as jnp
import numpy as np
from jax import lax
from jax.experimental import pallas as pl
from jax.experimental.pallas import tpu as pltpu

HEAD_DIM = 128
ROPE_DIM = HEAD_DIM // 4
ROPE_HALF = ROPE_DIM // 2
ROPE_THETA = 500000.0
NORM_EPS = 1e-6
ATTN_SCALE = HEAD_DIM ** -0.5
Q_BLOCK = 128

NSA_HEADS = 16
NSA_KV_HEADS = 4
NSA_GROUP = NSA_HEADS // NSA_KV_HEADS
NSA_WIDTH = NSA_HEADS * HEAD_DIM
KV_WIDTH = NSA_KV_HEADS * HEAD_DIM
CMP_LEN = 32
CMP_STRIDE = 16
SLC_LEN = 64
SLC_TOPK = 16
WIN_LEN = 512
CONV_K = 3

FOX_HEADS = 24
FOX_WIDTH = FOX_HEADS * HEAD_DIM
POOL_WINDOWS = (2, 4, 8, 16)
POOL_GROUP = 256
POOL_HALO = 16
CONV_HALO = 8

LANES = 128
MXU_WIDTH = 256
VMEM_LIMIT_BYTES = 56 * 1024 * 1024
BF16 = jnp.bfloat16
F32 = jnp.float32
NEG_INF = float("-inf")
LOG2E = 1.4426950408889634
Q_SCALE = ATTN_SCALE * LOG2E
SEL_MASK = -1e30
KV_UNROLL = 8
CMP_WIDTH_VARIANTS = 4
CMP_Q_TILE = 256


def _params(sem):
    return pltpu.CompilerParams(dimension_semantics=sem, vmem_limit_bytes=VMEM_LIMIT_BYTES)


def _tile(dim, pref):
    if dim <= pref:
        return dim
    t = pref
    while dim % t:
        t //= 2
    return t


def _rmsnorm_kernel(x_ref, g_ref, o_ref):
    x = x_ref[...]
    ms = jnp.mean(x * x, axis=-1, keepdims=True)
    o_ref[...] = (x * lax.rsqrt(ms + NORM_EPS) * g_ref[...]).astype(o_ref.dtype)


def rmsnorm_bf16(x, g):
    s, d = x.shape
    tm = _tile(s, 256)
    return pl.pallas_call(
        _rmsnorm_kernel,
        grid=(s // tm,),
        in_specs=[pl.BlockSpec((tm, d), lambda i: (i, 0)), pl.BlockSpec((1, d), lambda i: (0, 0))],
        out_specs=pl.BlockSpec((tm, d), lambda i: (i, 0)),
        out_shape=jax.ShapeDtypeStruct((s, d), BF16),
        compiler_params=_params(("parallel",)),
        name="rmsnorm",
    )(x, g.reshape(1, d))


def _mm_kernel(*refs, nk, n_extra, epilogue):
    a_ref, b_ref = refs[0], refs[1]
    extra = refs[2:2 + n_extra]
    o_ref = refs[2 + n_extra]
    part = jnp.dot(a_ref[...], b_ref[...].astype(BF16), preferred_element_type=F32)
    if nk == 1:
        o_ref[...] = epilogue(part, *extra).astype(o_ref.dtype)
        return
    acc_ref = refs[3 + n_extra]
    k = pl.program_id(2)

    @pl.when(k == 0)
    def _():
        acc_ref[...] = part

    @pl.when(k > 0)
    def _():
        acc_ref[...] += part

    @pl.when(k == nk - 1)
    def _():
        o_ref[...] = epilogue(acc_ref[...], *extra).astype(o_ref.dtype)


class Weight(NamedTuple):
    arr: jax.Array
    layer: Optional[int] = None
    col0: int = 0
    n: Optional[int] = None

    @property
    def k(self):
        return self.arr.shape[-2]

    @property
    def cols(self):
        return self.n if self.n is not None else self.arr.shape[-1] - self.col0

    def spec(self, tk, tn, k_of, j_of):
        assert self.col0 % tn == 0
        j0 = self.col0 // tn
        if self.layer is None:
            return pl.BlockSpec((tk, tn), lambda *g: (k_of(*g), j0 + j_of(*g)))
        return pl.BlockSpec((None, tk, tn), lambda *g: (self.layer, k_of(*g), j0 + j_of(*g)))


def _lhs_spec(block, index_map, single_buffer):
    if single_buffer:
        return pl.BlockSpec(block, index_map, pipeline_mode=pl.Buffered(1))
    return pl.BlockSpec(block, index_map)


def matmul(a, b, *, out_dtype, epilogue=None, extras=(), tm=1024, tn=512, tk=4096, single_buffer_a=False,
           name="matmul"):
    if not isinstance(b, Weight):
        b = Weight(b)
    m, kd = a.shape
    n = b.cols
    tm, tn, tk = _tile(m, tm), _tile(n, tn), _tile(kd, tk)
    nk = kd // tk
    if epilogue is None:
        epilogue = lambda acc: acc
    in_specs = [_lhs_spec((tm, tk), lambda i, j, k: (i, k), single_buffer_a and nk == 1),
                b.spec(tk, tn, lambda i, j, k: k, lambda i, j, k: j)]
    in_specs += [pl.BlockSpec(bs, im) for _, bs, im in extras]
    scratch = [pltpu.VMEM((tm, tn), F32)] if nk > 1 else []
    return pl.pallas_call(
        functools.partial(_mm_kernel, nk=nk, n_extra=len(extras), epilogue=epilogue),
        grid=(m // tm, n // tn, nk),
        in_specs=in_specs,
        out_specs=pl.BlockSpec((tm, tn), lambda i, j, k: (i, j)),
        out_shape=jax.ShapeDtypeStruct((m, n), out_dtype),
        scratch_shapes=scratch,
        compiler_params=_params(("parallel", "parallel", "arbitrary")),
        name=name,
    )(a, b.arr, *[e[0] for e in extras])


def _row_extra(arr, tn):
    return (arr, (1, tn), lambda i, j, k: (0, j))


def _tile_extra(arr, tm, tn):
    return (arr, (tm, tn), lambda i, j, k: (i, j))


def _head_proj_kernel(*refs, norm, rope, scale):
    a_ref, b_ref, o_ref = refs[0], refs[1], refs[-1]
    extra = list(refs[2:-1])
    g_ref = extra.pop(0) if norm else None
    if rope:
        cos_ref, sa_ref, sb_ref = extra
    a = a_ref[...]
    tn = b_ref.shape[1]
    piece = min(MXU_WIDTH, tn)
    for p0 in range(0, tn, piece):
        acc = jnp.dot(a, b_ref[:, p0:p0 + piece], preferred_element_type=F32)
        for c0 in range(0, piece, HEAD_DIM):
            blk = acc[:, c0:c0 + HEAD_DIM]
            cols = slice(p0 + c0, p0 + c0 + HEAD_DIM)
            if norm:
                ms = jnp.mean(blk * blk, axis=-1, keepdims=True)
                blk = blk * lax.rsqrt(ms + NORM_EPS) * g_ref[:, cols]
            if rope:
                blk = (blk * cos_ref[...] + pltpu.roll(blk, HEAD_DIM - ROPE_HALF, 1) * sa_ref[...]
                       + pltpu.roll(blk, ROPE_HALF, 1) * sb_ref[...])
            if scale != 1.0:
                blk = blk * scale
            o_ref[:, cols] = blk.astype(o_ref.dtype)


def head_proj(h, w, *, gain=None, rope=None, scale=1.0, name):
    m, kd = h.shape
    n = w.cols
    tm, tn = _tile(m, 1024), _tile(n, 1024)
    arrays, specs = [], []
    if gain is not None:
        arrays.append(gain.reshape(1, n))
        specs.append(pl.BlockSpec((1, tn), lambda i, j: (0, j)))
    if rope is not None:
        arrays += list(rope)
        specs += [pl.BlockSpec((tm, HEAD_DIM), lambda i, j: (i, 0))] * len(rope)
    return pl.pallas_call(
        functools.partial(_head_proj_kernel, norm=gain is not None, rope=rope is not None, scale=scale),
        grid=(m // tm, n // tn),
        in_specs=[pl.BlockSpec((tm, kd), lambda i, j: (i, 0)), w.spec(kd, tn, lambda i, j: 0, lambda i, j: j)] + specs,
        out_specs=pl.BlockSpec((tm, tn), lambda i, j: (i, j)),
        out_shape=jax.ShapeDtypeStruct((m, n), BF16),
        compiler_params=_params(("parallel", "parallel")),
        name=name,
    )(h, w.arr, *arrays)


def _rope_table_kernel(pos_ref, freq_ref, cos_ref, sa_ref, sb_ref):
    ang = pos_ref[...] * freq_ref[...]
    lane = lax.broadcasted_iota(jnp.int32, ang.shape, 1)
    c, s = jnp.cos(ang), jnp.sin(ang)
    cos_ref[...] = jnp.where(lane < ROPE_DIM, c, 1.0)
    sa_ref[...] = jnp.where(lane < ROPE_HALF, -s, 0.0)
    sb_ref[...] = jnp.where((lane >= ROPE_HALF) & (lane < ROPE_DIM), s, 0.0)


def rope_tables(positions):
    s = positions.shape[0]
    inv_freq = ROPE_THETA ** (-jnp.arange(ROPE_HALF, dtype=F32) / ROPE_HALF)
    freq_row = jnp.zeros((HEAD_DIM,), F32).at[:ROPE_DIM].set(jnp.concatenate([inv_freq, inv_freq]))
    pos_rep = jnp.broadcast_to(positions.astype(F32)[:, None], (s, HEAD_DIM))
    tm = _tile(s, 1024)
    spec = pl.BlockSpec((tm, HEAD_DIM), lambda i: (i, 0))
    shp = jax.ShapeDtypeStruct((s, HEAD_DIM), F32)
    return pl.pallas_call(
        _rope_table_kernel,
        grid=(s // tm,),
        in_specs=[spec, pl.BlockSpec((1, HEAD_DIM), lambda i: (0, 0))],
        out_specs=[spec, spec, spec],
        out_shape=[shp, shp, shp],
        compiler_params=_params(("parallel",)),
        name="rope_tables",
    )(pos_rep, freq_row.reshape(1, HEAD_DIM))


def _gelu_tanh(x):
    return 0.5 * x * (1.0 + jnp.tanh(np.sqrt(2.0 / np.pi).astype(np.float32) * (x + 0.044715 * (x * x * x))))


def _compress_kernel(a_ref, w_ref, pe_ref, w2_ref, g_ref, o_ref, acc_ref, bias_ref, *, norm):
    l = pl.program_id(0)
    nl = pl.num_programs(0)
    a = a_ref[...]
    w = w_ref[0]
    pe_part = jnp.dot(pe_ref[0], w, preferred_element_type=F32)

    @pl.when(l == 0)
    def _():
        bias_ref[...] = pe_part
        for g in range(NSA_KV_HEADS):
            acc_ref[g] = jnp.dot(a[:, g * HEAD_DIM:(g + 1) * HEAD_DIM], w, preferred_element_type=F32)

    @pl.when(l > 0)
    def _():
        bias_ref[...] += pe_part
        for g in range(NSA_KV_HEADS):
            acc_ref[g] += jnp.dot(a[:, g * HEAD_DIM:(g + 1) * HEAD_DIM], w, preferred_element_type=F32)

    @pl.when(l == nl - 1)
    def _():
        n_chunk = a.shape[0]
        bias = bias_ref[0:1, :HEAD_DIM] + bias_ref[1:2, HEAD_DIM:]
        for g in range(NSA_KV_HEADS):
            p = acc_ref[g]
            hid = p[:, :HEAD_DIM] + pltpu.roll(p[:, HEAD_DIM:], n_chunk - 1, 0) + bias
            out = jnp.dot(_gelu_tanh(hid).astype(BF16), w2_ref[...], preferred_element_type=F32)
            if norm:
                ms = jnp.mean(out * out, axis=-1, keepdims=True)
                out = out * lax.rsqrt(ms + NORM_EPS) * g_ref[...]
            o_ref[:, g * HEAD_DIM:(g + 1) * HEAD_DIM] = out.astype(o_ref.dtype)


def compress(kv, pe, w1, w2, gain):
    s = kv.shape[0]
    n_chunk = s // CMP_STRIDE
    a = kv.reshape(n_chunk, CMP_STRIDE * KV_WIDTH)
    r = CMP_LEN // CMP_STRIDE
    w1r = w1.reshape(r, CMP_STRIDE, HEAD_DIM, HEAD_DIM)
    wcat = jnp.concatenate([w1r[0], w1r[1]], axis=-1).astype(BF16)
    pe_r = pe.reshape(r, CMP_STRIDE, HEAD_DIM).transpose(1, 0, 2)
    pe_l = jnp.zeros((CMP_STRIDE, 8, HEAD_DIM), F32).at[:, :r].set(pe_r).astype(BF16)
    norm = gain is not None
    g = (gain if norm else jnp.ones((HEAD_DIM,), F32)).reshape(1, HEAD_DIM)
    return pl.pallas_call(
        functools.partial(_compress_kernel, norm=norm),
        grid=(CMP_STRIDE,),
        in_specs=[
            pl.BlockSpec((n_chunk, KV_WIDTH), lambda l: (0, l)),
            pl.BlockSpec((1, HEAD_DIM, 2 * HEAD_DIM), lambda l: (l, 0, 0)),
            pl.BlockSpec((1, 8, HEAD_DIM), lambda l: (l, 0, 0)),
            pl.BlockSpec((HEAD_DIM, HEAD_DIM), lambda l: (0, 0)),
            pl.BlockSpec((1, HEAD_DIM), lambda l: (0, 0)),
        ],
        out_specs=pl.BlockSpec((n_chunk, KV_WIDTH), lambda l: (0, 0)),
        out_shape=jax.ShapeDtypeStruct((n_chunk, KV_WIDTH), BF16),
        scratch_shapes=[pltpu.VMEM((NSA_KV_HEADS, n_chunk, 2 * HEAD_DIM), F32), pltpu.VMEM((8, 2 * HEAD_DIM), F32)],
        compiler_params=_params(("arbitrary",)),
        name="nsa_compress",
    )(a, wcat, pe_l, w2.astype(BF16), g)


def _stack_heads(qb):
    return jnp.concatenate([qb[:, j * HEAD_DIM:(j + 1) * HEAD_DIM] for j in range(NSA_GROUP)], axis=0)


def _cmp_body(i, q_ref, k_ref, v_ref, ov_ref, o_ref, sel_ref, width, nbw):
    tq = q_ref.shape[0]
    nb = sel_ref.shape[-1]
    q4 = _stack_heads(q_ref[...])
    s = lax.dot_general(q4, k_ref[:width, :], (((1,), (1,)), ((), ())), preferred_element_type=F32)
    rows = lax.broadcasted_iota(jnp.int32, (NSA_GROUP * tq, width), 0)
    cols = lax.broadcasted_iota(jnp.int32, (NSA_GROUP * tq, width), 1)
    tpos = i * tq + rows % tq
    s = jnp.where(cols * CMP_STRIDE + (CMP_LEN - 1) <= tpos, s, NEG_INF)
    m = jnp.max(s, axis=-1, keepdims=True)
    m = jnp.where(m == NEG_INF, 0.0, m)
    e = jnp.exp2(s - m)
    p = e * (1.0 / jnp.maximum(jnp.sum(e, axis=-1, keepdims=True), 1e-30))
    o = jnp.dot(p.astype(BF16), v_ref[:width, :], preferred_element_type=F32)
    for j in range(NSA_GROUP):
        o_ref[:, j * HEAD_DIM:(j + 1) * HEAD_DIM] = o[j * tq:(j + 1) * tq]

    psum = p[0:tq] + p[tq:2 * tq] + p[2 * tq:3 * tq] + p[3 * tq:4 * tq]
    p_hi = psum.astype(BF16)
    p_lo = (psum - p_hi.astype(F32)).astype(BF16)
    ov = ov_ref[:width, :nbw]
    imp = jnp.dot(p_hi, ov, preferred_element_type=F32) + jnp.dot(p_lo, ov, preferred_element_type=F32)

    blk = lax.broadcasted_iota(jnp.int32, (tq, nbw), 1)
    t = i * tq + lax.broadcasted_iota(jnp.int32, (tq, nbw), 0)
    cur = t // SLC_LEN
    forced = (blk == 0) | (blk == cur) | (blk == cur - 1)
    valid = blk * SLC_LEN <= t
    work = jnp.where(forced, jnp.inf, jnp.where(valid, imp, NEG_INF))
    sel = jnp.zeros((tq, nbw), F32)
    n_valid = jnp.sum(jnp.where(work > NEG_INF, 1.0, 0.0), axis=-1, keepdims=True)
    for r in range(min(SLC_TOPK, nb)):
        first = jnp.argmax(work, axis=-1, keepdims=True)
        pick = (blk == first) & (n_valid > r)
        sel = jnp.where(pick, 1.0, sel)
        work = jnp.where(pick, NEG_INF, work)
    bias = jnp.where(sel > 0.5, 0.0, SEL_MASK)
    if nbw < nb:
        bias = jnp.concatenate([bias, jnp.full((tq, nb - nbw), SEL_MASK, F32)], axis=1)
    sel_ref[...] = bias.astype(sel_ref.dtype)


def _cmp_kernel(q_ref, k_ref, v_ref, ov_ref, o_ref, sel_ref, *, col_chunk):
    i = pl.program_id(1)
    tq = q_ref.shape[0]
    n_chunk = k_ref.shape[0]
    nb = sel_ref.shape[-1]
    n_var = n_chunk // col_chunk
    need = (i * tq + tq - CMP_LEN) // CMP_STRIDE + 1
    variant = jnp.minimum((need - 1) // col_chunk, n_var - 1)
    for v in range(n_var):
        width = (v + 1) * col_chunk
        i_max = (CMP_STRIDE * width + CMP_LEN - 1) // tq - 1
        blocks = ((i_max + 1) * tq - 1) // SLC_LEN + 1
        nbw = min(nb, -(-blocks // LANES) * LANES)

        @pl.when(variant == v)
        def _(width=width, nbw=nbw):
            _cmp_body(i, q_ref, k_ref, v_ref, ov_ref, o_ref, sel_ref, width, nbw)


def nsa_cmp_and_select(q, k_cmp, v_cmp):
    s = q.shape[0]
    n_chunk = k_cmp.shape[0]
    nb = s // SLC_LEN
    tq = _tile(s, CMP_Q_TILE)
    c_start = np.arange(n_chunk) * CMP_STRIDE
    s_start = np.arange(nb) * SLC_LEN
    overlap = np.clip(np.minimum(c_start[:, None] + CMP_LEN, s_start[None, :] + SLC_LEN)
                      - np.maximum(c_start[:, None], s_start[None, :]), 0, None).astype(np.float32) / CMP_LEN
    gw = NSA_GROUP * HEAD_DIM
    col_chunk = max(LANES, n_chunk // CMP_WIDTH_VARIANTS)
    return pl.pallas_call(
        functools.partial(_cmp_kernel, col_chunk=col_chunk),
        grid=(NSA_KV_HEADS, s // tq),
        in_specs=[
            pl.BlockSpec((tq, gw), lambda g, i: (i, g)),
            pl.BlockSpec((n_chunk, HEAD_DIM), lambda g, i: (0, g)),
            pl.BlockSpec((n_chunk, HEAD_DIM), lambda g, i: (0, g)),
            pl.BlockSpec((n_chunk, nb), lambda g, i: (0, 0)),
        ],
        out_specs=[
            pl.BlockSpec((tq, gw), lambda g, i: (i, g)),
            pl.BlockSpec((None, tq, nb), lambda g, i: (g, i, 0)),
        ],
        out_shape=[jax.ShapeDtypeStruct((s, NSA_WIDTH), F32), jax.ShapeDtypeStruct((NSA_KV_HEADS, s, nb), BF16)],
        compiler_params=_params(("parallel", "parallel")),
        name="nsa_cmp_select",
    )(q, k_cmp, v_cmp, jnp.asarray(overlap, BF16))


def _flash_tile(q2, k2, v, m_ref, acc_ref, mask=None):
    tk = k2.shape[0]
    s = lax.dot_general(q2, k2, (((1,), (1,)), ((), ())), preferred_element_type=F32)
    if mask is not None:
        s = jnp.where(mask, s, NEG_INF)
    m_old = m_ref[...]
    m_new = jnp.maximum(m_old, jnp.max(s, axis=-1, keepdims=True))
    alpha = jnp.exp2(m_old - m_new)
    p = jnp.exp2(s - jnp.concatenate([m_new] * (tk // LANES), axis=1))
    v2 = jnp.concatenate([v, jnp.ones((tk, LANES), v.dtype)], axis=1)
    acc_ref[...] = jnp.concatenate([alpha, alpha], axis=1) * acc_ref[...] + jnp.dot(
        p.astype(BF16), v2, preferred_element_type=F32)
    m_ref[...] = m_new


def _init_flash(m_ref, acc_ref):
    m_ref[...] = jnp.full(m_ref.shape, NEG_INF, F32)
    acc_ref[...] = jnp.zeros(acc_ref.shape, F32)


def _flash_result(acc_ref):
    acc = acc_ref[...]
    return acc[:, :HEAD_DIM] * (1.0 / acc[:, HEAD_DIM:])


def _for_each_tile(n, tile_fn):
    def body(jj, carry):
        for u in range(KV_UNROLL):
            tile_fn(jj * KV_UNROLL + u)
        return carry

    lax.fori_loop(0, n // KV_UNROLL, body, 0)
    base = (n // KV_UNROLL) * KV_UNROLL
    g = KV_UNROLL // 2
    while g >= 1:
        take = (n & g) != 0

        @pl.when(take)
        def _(base=base, g=g):
            for u in range(g):
                tile_fn(base + u)

        base = base + jnp.where(take, g, 0)
        g //= 2


def _slc_kernel(q_ref, bias_ref, k_ref, kx_ref, v_ref, o_ref, q2_ref, m_ref, acc_ref, *, tk):
    i = pl.program_id(1)
    tq = q_ref.shape[0]
    cw = kx_ref.shape[1]
    _init_flash(m_ref, acc_ref)
    q4 = _stack_heads(q_ref[...])
    for c in range(q2_ref.shape[0]):
        bias = bias_ref[:, c * cw:(c + 1) * cw]
        q2_ref[c] = jnp.concatenate([q4, jnp.concatenate([bias] * NSA_GROUP, axis=0)], axis=1)

    def tile(j, masked):
        off = pl.multiple_of(j * tk, tk)
        k2 = jnp.concatenate([k_ref[pl.ds(off, tk), :], kx_ref[pl.ds(off, tk), :]], axis=1)
        q2 = q2_ref[(j * tk) // (SLC_LEN * cw)]
        mask = None
        if masked:
            rows = lax.broadcasted_iota(jnp.int32, (NSA_GROUP * tq, tk), 0)
            cols = lax.broadcasted_iota(jnp.int32, (NSA_GROUP * tq, tk), 1)
            mask = off + cols <= i * tq + rows % tq
        _flash_tile(q2, k2, v_ref[pl.ds(off, tk), :], m_ref, acc_ref, mask)

    last = (i * tq) // tk
    _for_each_tile(last, lambda j: tile(j, False))
    tile(last, True)
    o = _flash_result(acc_ref)
    for h in range(NSA_GROUP):
        o_ref[:, h * HEAD_DIM:(h + 1) * HEAD_DIM] = o[h * tq:(h + 1) * tq]


def nsa_selected(q, k_slc, v_slc, sel_bias):
    s = q.shape[0]
    nb = s // SLC_LEN
    cw = min(nb, LANES)
    tq, tk = Q_BLOCK, _tile(s, 512)
    key_blk = jnp.arange(s, dtype=jnp.int32)[:, None] // SLC_LEN
    key_onehot = (key_blk % cw == jnp.arange(cw, dtype=jnp.int32)[None, :]).astype(BF16)
    gw = NSA_GROUP * HEAD_DIM
    rows = NSA_GROUP * tq
    return pl.pallas_call(
        functools.partial(_slc_kernel, tk=tk),
        grid=(NSA_KV_HEADS, s // tq),
        in_specs=[
            pl.BlockSpec((tq, gw), lambda g, i: (i, g)),
            pl.BlockSpec((None, tq, nb), lambda g, i: (g, i, 0)),
            pl.BlockSpec((s, HEAD_DIM), lambda g, i: (0, g)),
            pl.BlockSpec((s, cw), lambda g, i: (0, 0)),
            pl.BlockSpec((s, HEAD_DIM), lambda g, i: (0, g)),
        ],
        out_specs=pl.BlockSpec((tq, gw), lambda g, i: (i, g)),
        out_shape=jax.ShapeDtypeStruct((s, NSA_WIDTH), F32),
        scratch_shapes=[pltpu.VMEM((nb // cw, rows, HEAD_DIM + cw), BF16), pltpu.VMEM((rows, LANES), F32),
                        pltpu.VMEM((rows, 2 * HEAD_DIM), F32)],
        compiler_params=_params(("parallel", "arbitrary")),
        name="nsa_selected",
    )(q, sel_bias, k_slc, key_onehot, v_slc)


WIN_TILES = WIN_LEN // Q_BLOCK + 1


def _win_kernel(*refs):
    q_ref = refs[0]
    k_refs = refs[1:1 + WIN_TILES]
    v_refs = refs[1 + WIN_TILES:1 + 2 * WIN_TILES]
    ocmp_ref, oslc_ref, gate_ref, o_ref = refs[1 + 2 * WIN_TILES:]
    i = pl.program_id(0)
    tq = q_ref.shape[0]
    q4 = _stack_heads(q_ref[...])
    kcat = jnp.concatenate([r[...] for r in k_refs], axis=0)
    vcat = jnp.concatenate([r[...] for r in v_refs], axis=0)
    s = lax.dot_general(q4, kcat, (((1,), (1,)), ((), ())), preferred_element_type=F32)
    rows = lax.broadcasted_iota(jnp.int32, s.shape, 0)
    cols = lax.broadcasted_iota(jnp.int32, s.shape, 1)
    tpos = i * tq + rows % tq
    wpos = (i - (WIN_TILES - 1)) * tq + cols
    ok = (wpos <= tpos) & (wpos > tpos - WIN_LEN) & (wpos >= 0)
    s = jnp.where(ok, s, NEG_INF)
    m = jnp.max(s, axis=-1, keepdims=True)
    e = jnp.exp2(s - m)
    p = e * (1.0 / jnp.sum(e, axis=-1, keepdims=True))
    o_win = jnp.dot(p.astype(BF16), vcat, preferred_element_type=F32)
    gates = gate_ref[...]
    for h in range(NSA_GROUP):
        sl = slice(h * HEAD_DIM, (h + 1) * HEAD_DIM)
        o = (gates[:, 3 * h:3 * h + 1] * ocmp_ref[:, sl] + gates[:, 3 * h + 1:3 * h + 2] * oslc_ref[:, sl]
             + gates[:, 3 * h + 2:3 * h + 3] * o_win[h * tq:(h + 1) * tq])
        o_ref[:, sl] = o.astype(o_ref.dtype)


def nsa_window_combine(q, k_win, v_win, o_cmp, o_slc, gates, out_width):
    s = q.shape[0]
    tq = Q_BLOCK
    gw = NSA_GROUP * HEAD_DIM
    back = WIN_TILES - 1
    kv_specs = [pl.BlockSpec((tq, HEAD_DIM), lambda i, g, d=d: (jnp.maximum(i - back + d, 0), g))
                for d in range(WIN_TILES)]
    blk = pl.BlockSpec((tq, gw), lambda i, g: (i, g))
    return pl.pallas_call(
        _win_kernel,
        grid=(s // tq, NSA_KV_HEADS),
        in_specs=[blk] + kv_specs + kv_specs + [blk, blk, pl.BlockSpec((None, tq, 3 * NSA_GROUP), lambda i, g: (g, i, 0))],
        out_specs=blk,
        out_shape=jax.ShapeDtypeStruct((s, out_width), BF16),
        compiler_params=_params(("parallel", "parallel")),
        name="nsa_window_combine",
    )(q, *([k_win] * WIN_TILES), *([v_win] * WIN_TILES), o_cmp, o_slc, gates)


def _conv_kernel(b_ref, c_ref, h_ref, ch_ref, hh_ref, w_ref, dst_ref, o_ref):
    del dst_ref
    i = pl.program_id(0)
    u = c_ref[...] * h_ref[...]
    halo = jnp.where(i > 0, ch_ref[...] * hh_ref[...], 0.0)
    x = jnp.concatenate([halo, u], axis=0)
    w = w_ref[...]
    y = (w[2:3] * x + w[1:2] * pltpu.roll(x, 1, 0) + w[0:1] * pltpu.roll(x, 2, 0))[CONV_HALO:]
    o_ref[...] = (b_ref[...] * y).astype(o_ref.dtype)


def short_conv(z, conv_w, dst):
    s = z.shape[0]
    cw = conv_w.shape[1]
    tm, tc = _tile(s, 512), _tile(cw, 512)
    nc = cw // tc
    hb = tm // CONV_HALO
    col0 = dst.shape[1] - cw
    assert col0 % tc == 0
    halo = lambda off: pl.BlockSpec((CONV_HALO, tc), lambda i, j: (jnp.maximum(i * hb - 1, 0), off * nc + j))
    main = lambda off: pl.BlockSpec((tm, tc), lambda i, j: (i, off * nc + j))
    w8 = jnp.zeros((8, cw), F32).at[:CONV_K].set(conv_w)
    return pl.pallas_call(
        _conv_kernel,
        grid=(s // tm, nc),
        in_specs=[main(0), main(1), main(2), halo(1), halo(2), pl.BlockSpec((8, tc), lambda i, j: (0, j)),
                  pl.BlockSpec(memory_space=pl.ANY)],
        out_specs=pl.BlockSpec((tm, tc), lambda i, j: (i, col0 // tc + j)),
        out_shape=jax.ShapeDtypeStruct(dst.shape, dst.dtype),
        input_output_aliases={6: 0},
        compiler_params=_params(("parallel", "parallel")),
        name="short_conv",
    )(z, z, z, z, z, w8, dst)


def _split3(x):
    hi = x.astype(BF16)
    r1 = x - hi.astype(F32)
    mid = r1.astype(BF16)
    lo = (r1 - mid.astype(F32)).astype(BF16)
    return hi, mid, lo


def _cumsum_kernel(x_ref, hi_ref, mid_ref, lo_ref, carry_ref):
    @pl.when(pl.program_id(0) == 0)
    def _():
        carry_ref[...] = jnp.zeros(carry_ref.shape, F32)

    x = x_ref[...]
    t = x.shape[0]
    tri = (lax.broadcasted_iota(jnp.int32, (t, t), 0) >= lax.broadcasted_iota(jnp.int32, (t, t), 1)).astype(BF16)
    c = sum(jnp.dot(tri, part, preferred_element_type=F32) for part in _split3(x)) + carry_ref[0:1]
    carry_ref[...] = jnp.broadcast_to(c[t - 1:t], carry_ref.shape)
    hi_ref[...], mid_ref[...], lo_ref[...] = _split3(c * LOG2E)


def cumsum_log2_split(x):
    s, w = x.shape
    t = _tile(s, 512)
    spec = pl.BlockSpec((t, w), lambda i: (i, 0))
    shp = jax.ShapeDtypeStruct((s, w), BF16)
    return pl.pallas_call(
        _cumsum_kernel,
        grid=(s // t,),
        in_specs=[spec],
        out_specs=[spec, spec, spec],
        out_shape=[shp, shp, shp],
        scratch_shapes=[pltpu.VMEM((8, w), F32)],
        compiler_params=_params(("arbitrary",)),
        name="cumsum",
    )(x)


def _fox_kernel(q_ref, qx_ref, k_ref, kx_ref, v_ref, o_ref, m_ref, acc_ref, *, t):
    i = pl.program_id(1)
    _init_flash(m_ref, acc_ref)
    q2 = jnp.concatenate([q_ref[...], qx_ref[...]], axis=1)

    def tile(j, masked):
        off = pl.multiple_of(j * t, t)
        k2 = jnp.concatenate([k_ref[pl.ds(off, t), :], kx_ref[pl.ds(off, t), :]], axis=1)
        mask = None
        if masked:
            mask = lax.broadcasted_iota(jnp.int32, (t, t), 1) <= lax.broadcasted_iota(jnp.int32, (t, t), 0)
        _flash_tile(q2, k2, v_ref[pl.ds(off, t), :], m_ref, acc_ref, mask)

    _for_each_tile(i, lambda j: tile(j, False))
    tile(i, True)
    o_ref[...] = _flash_result(acc_ref).astype(o_ref.dtype)


def fox_attention(q, k, v, c_parts, out_width):
    s = q.shape[0]
    t = _tile(s, 512)
    parts = jnp.stack([part[:, :FOX_HEADS] for part in c_parts], axis=-1)
    ones = jnp.ones_like(parts)
    widen = lambda a: jnp.pad(a, ((0, 0), (0, 0), (0, HEAD_DIM - a.shape[-1]))).reshape(s, FOX_WIDTH)
    qx = widen(jnp.concatenate([parts, ones], axis=-1))
    kx = widen(jnp.concatenate([ones, -parts], axis=-1))
    tile_spec = pl.BlockSpec((t, HEAD_DIM), lambda h, i: (i, h))
    full_spec = pl.BlockSpec((s, HEAD_DIM), lambda h, i: (0, h))
    return pl.pallas_call(
        functools.partial(_fox_kernel, t=t),
        grid=(FOX_HEADS, s // t),
        in_specs=[tile_spec, tile_spec, full_spec, full_spec, full_spec],
        out_specs=tile_spec,
        out_shape=jax.ShapeDtypeStruct((s, out_width), BF16),
        scratch_shapes=[pltpu.VMEM((t, LANES), F32), pltpu.VMEM((t, 2 * HEAD_DIM), F32)],
        compiler_params=_params(("parallel", "arbitrary")),
        name="fox_attention",
    )(q, qx, k, kx, v)


def _pool_kernel(u_ref, halo_ref, w_ref, scale_ref, dst_ref, o_ref):
    del dst_ref
    i = pl.program_id(0)
    tm = u_ref.shape[0]
    u = u_ref[...]
    halo = jnp.where(i > 0, halo_ref[...], 0.0)
    x = jnp.concatenate([halo, u], axis=0)
    t1 = i * tm + lax.broadcasted_iota(jnp.int32, (tm, POOL_GROUP), 0) + 1
    for g, win in enumerate(POOL_WINDOWS):
        sl = slice(g * POOL_GROUP, (g + 1) * POOL_GROUP)
        acc = x[:, sl]
        span = 1
        while span < win:
            acc = acc + pltpu.roll(acc, span, 0)
            span *= 2
        cnt = jnp.minimum(t1, win).astype(F32)
        d = acc[POOL_HALO:] / cnt - u[:, sl]
        y = jnp.dot(d.astype(BF16), w_ref[g], preferred_element_type=F32)
        o_ref[:, sl] = (y * scale_ref[:, sl]).astype(o_ref.dtype)


def multiscale_pool(u, w_pool, scale, dst):
    s, pw = u.shape
    tm = _tile(s, 512)
    hb = tm // POOL_HALO
    col0 = dst.shape[1] - pw
    assert col0 % pw == 0
    return pl.pallas_call(
        _pool_kernel,
        grid=(s // tm,),
        in_specs=[
            pl.BlockSpec((tm, pw), lambda i: (i, 0)),
            pl.BlockSpec((POOL_HALO, pw), lambda i: (jnp.maximum(i * hb - 1, 0), 0)),
            pl.BlockSpec(w_pool.shape, lambda i: (0, 0, 0)),
            pl.BlockSpec((1, pw), lambda i: (0, 0)),
            pl.BlockSpec(memory_space=pl.ANY),
        ],
        out_specs=pl.BlockSpec((tm, pw), lambda i: (i, col0 // pw)),
        out_shape=jax.ShapeDtypeStruct(dst.shape, dst.dtype),
        input_output_aliases={4: 0},
        compiler_params=_params(("parallel",)),
        name="multiscale_pool",
    )(u, u, w_pool.astype(BF16), scale.reshape(1, pw), dst)


def _swiglu_kernel(h_ref, w1_ref, w3_ref, o_ref):
    h = h_ref[...]
    a = jnp.dot(h, w1_ref[...].astype(BF16), preferred_element_type=F32)
    b = jnp.dot(h, w3_ref[...].astype(BF16), preferred_element_type=F32)
    o_ref[...] = (a * jax.nn.sigmoid(a) * b).astype(o_ref.dtype)


def swiglu_up(h, w1, w3):
    m, d = h.shape
    n = w1.cols
    tm, tn = _tile(m, 2048), _tile(n, 512)
    specs = [w.spec(d, tn, lambda i, j: 0, lambda i, j: j) for w in (w1, w3)]
    return pl.pallas_call(
        _swiglu_kernel,
        grid=(m // tm, n // tn),
        in_specs=[_lhs_spec((tm, d), lambda i, j: (i, 0), True)] + specs,
        out_specs=pl.BlockSpec((tm, tn), lambda i, j: (i, j)),
        out_shape=jax.ShapeDtypeStruct((m, n), BF16),
        compiler_params=_params(("parallel", "parallel")),
        name="swiglu_up",
    )(h, w1.arr, w3.arr)


def _resid_epilogue(acc, x_ref):
    return x_ref[...] + acc


def _gate_epilogue(acc, x_ref, p_ref, wp_ref):
    return x_ref[...] + jax.nn.sigmoid(acc) * jnp.dot(p_ref[...], wp_ref[...].astype(BF16),
                                                      preferred_element_type=F32)


def _logsig_epilogue(acc, b_ref):
    y = -(acc + b_ref[...])
    return -(jnp.maximum(y, 0.0) + jnp.log1p(jnp.exp(-jnp.abs(y))))


def _sigmoid_epilogue(acc):
    return jax.nn.sigmoid(acc)


def matmul_resid(a, b, x, *, tm=1024, tn=512, single_buffer_a=False, name):
    tm, tn = _tile(a.shape[0], tm), _tile(b.cols, tn)
    return matmul(a, b, out_dtype=F32, epilogue=_resid_epilogue, extras=[_tile_extra(x, tm, tn)], tm=tm, tn=tn,
                  tk=a.shape[1], single_buffer_a=single_buffer_a, name=name)


def ffn_and_embed(x, p_i, layer, f_norm, w1, w3, w2, e_norm, w_gate, w_proj):
    h = rmsnorm_bf16(x, f_norm)
    u = swiglu_up(h, Weight(w1, layer), Weight(w3, layer))
    x = matmul_resid(u, Weight(w2, layer), x, tm=512, tn=MXU_WIDTH, single_buffer_a=True, name="ffn_down")
    h = rmsnorm_bf16(x, e_norm)
    m, d = x.shape
    tm, tn = _tile(m, 1024), _tile(d, 512)
    pd = p_i.shape[1]
    extras = [_tile_extra(x, tm, tn), (p_i.astype(BF16), (tm, pd), lambda i, j, k: (i, 0)),
              (w_proj, (None, pd, tn), lambda i, j, k: (layer, 0, j))]
    return matmul(h, Weight(w_gate, layer), out_dtype=F32, epilogue=_gate_epilogue, extras=extras, tm=tm, tn=tn,
                  name="embed_gate")


def even_mixer(x, rope, layer, norm_g, w_in, q_norm, k_norm, cmp_pe, cmp_w1, cmp_w2, conv_w, w_out):
    s = x.shape[0]
    cw = conv_w.shape[1]
    h = rmsnorm_bf16(x, norm_g)
    o_kv = NSA_WIDTH
    o_gate = o_kv + 6 * KV_WIDTH
    o_conv = o_gate + 3 * NSA_HEADS
    kv_cols = lambda i: slice(o_kv + i * KV_WIDTH, o_kv + (i + 1) * KV_WIDTH)
    tile_gain = lambda g, n: jnp.tile(g, n // HEAD_DIM)

    q = head_proj(h, Weight(w_in, layer, 0, NSA_WIDTH), gain=tile_gain(q_norm, NSA_WIDTH), rope=rope, scale=Q_SCALE,
                  name="nsa_q_proj")
    k_cmp_in = head_proj(h, Weight(w_in, layer, o_kv, KV_WIDTH), rope=rope, name="nsa_kcmp_proj")
    w_v = jnp.concatenate([w_in[layer, :, kv_cols(i)] for i in (1, 3, 5)], axis=1)
    v_all = matmul(h, w_v, out_dtype=BF16, name="nsa_v_proj")
    v_cmp_in, v_slc, v_win = (v_all[:, i * KV_WIDTH:(i + 1) * KV_WIDTH] for i in range(3))
    w_k = jnp.concatenate([w_in[layer, :, kv_cols(i)] for i in (2, 4)], axis=1)
    g_k = jnp.concatenate([tile_gain(k_norm[1], KV_WIDTH), tile_gain(k_norm[2], KV_WIDTH)])
    k_both = head_proj(h, Weight(w_k), gain=g_k, rope=rope, name="nsa_k_proj")
    k_slc, k_win = k_both[:, :KV_WIDTH], k_both[:, KV_WIDTH:]
    gates = matmul(h, Weight(w_in, layer, o_gate, LANES), out_dtype=F32, epilogue=_sigmoid_epilogue,
                   name="nsa_gate_proj")[:, :3 * NSA_HEADS]
    gates = gates.reshape(s, NSA_KV_HEADS, 3 * NSA_GROUP).transpose(1, 0, 2)
    z_conv = matmul(h, w_in[layer, :, o_conv:], out_dtype=F32, name="conv_proj")

    k_cmp = compress(k_cmp_in, cmp_pe[0], cmp_w1[0], cmp_w2[0], k_norm[0])
    v_cmp = compress(v_cmp_in, cmp_pe[1], cmp_w1[1], cmp_w2[1], None)
    o_cmp, sel = nsa_cmp_and_select(q, k_cmp, v_cmp)
    o_slc = nsa_selected(q, k_slc, v_slc, sel)
    mixed = nsa_window_combine(q, k_win, v_win, o_cmp, o_slc, gates, NSA_WIDTH + cw)
    mixed = short_conv(z_conv, conv_w, mixed)
    return matmul_resid(mixed, Weight(w_out, layer), x, name="even_out_proj")


def odd_mixer(x, layer, norm_g, w_in, f_bias, q_norm, k_norm, pool_w, pool_scale, w_out):
    h = rmsnorm_bf16(x, norm_g)
    o_fgate = 3 * FOX_WIDTH
    o_pool = o_fgate + FOX_HEADS
    pw = pool_scale.shape[0]
    tile_gain = lambda g: jnp.tile(g, FOX_WIDTH // HEAD_DIM)
    q = head_proj(h, Weight(w_in, layer, 0, FOX_WIDTH), gain=tile_gain(q_norm), scale=Q_SCALE, name="fox_q_proj")
    k = head_proj(h, Weight(w_in, layer, FOX_WIDTH, FOX_WIDTH), gain=tile_gain(k_norm), name="fox_k_proj")
    v = matmul(h, Weight(w_in, layer, 2 * FOX_WIDTH, FOX_WIDTH), out_dtype=BF16, name="fox_v_proj")
    b_f = jnp.pad(f_bias, (0, LANES - FOX_HEADS)).reshape(1, LANES)
    log_f = matmul(h, Weight(w_in, layer, o_fgate, LANES), out_dtype=F32, epilogue=_logsig_epilogue,
                   extras=[_row_extra(b_f, LANES)], name="fox_gate_proj")
    u = matmul(h, w_in[layer, :, o_pool:], out_dtype=F32, name="pool_proj")
    mixed = fox_attention(q, k, v, cumsum_log2_split(log_f), FOX_WIDTH + pw)
    mixed = multiscale_pool(u, pool_w, pool_scale, mixed)
    return matmul_resid(mixed, Weight(w_out, layer), x, name="odd_out_proj")


def kernel(x, p, positions, a_norm, a_w_in, a_q_norm, a_k_norm, a_cmp_pe, a_cmp_w1, a_cmp_w2, a_conv_w, a_w_out, b_norm, b_w_in, b_f_bias, b_q_norm, b_k_norm, b_pool_w, b_pool_scale, b_w_out, f_norm, f_w1, f_w3, f_w2, e_norm, e_w_gate, e_w_proj):
    batch, s, d = x.shape
    depth = p.shape[0]
    a_w_in, b_w_in, f_w2 = a_w_in.astype(BF16), b_w_in.astype(BF16), f_w2.astype(BF16)
    outs = []
    for b in range(batch):
        xb = x[b]
        rope = rope_tables(positions[b])
        for i in range(depth):
            j = i // 2
            if i % 2 == 0:
                xb = even_mixer(xb, rope, j, a_norm[j], a_w_in, a_q_norm[j], a_k_norm[j], a_cmp_pe[j], a_cmp_w1[j],
                                a_cmp_w2[j], a_conv_w[j], a_w_out)
            else:
                xb = odd_mixer(xb, j, b_norm[j], b_w_in, b_f_bias[j], b_q_norm[j], b_k_norm[j], b_pool_w[j],
                               b_pool_scale[j], b_w_out)
            xb = ffn_and_embed(xb, p[i, b], i, f_norm[i], f_w1, f_w3, f_w2, e_norm[i], e_w_gate, e_w_proj)
        outs.append(xb)
    return jnp.stack(outs, axis=0)
```

```python
import functools
from typing import NamedTuple, Optional

import jax
import jax.numpy as jnp
import numpy as np
from jax import lax
from jax.experimental import pallas as pl
from jax.experimental.pallas import tpu as pltpu

HEAD_DIM = 128
ROPE_DIM = HEAD_DIM // 4
ROPE_HALF = ROPE_DIM // 2
ROPE_THETA = 500000.0
NORM_EPS = 1e-6
ATTN_SCALE = HEAD_DIM ** -0.5
Q_BLOCK = 128

NSA_HEADS = 16
NSA_KV_HEADS = 4
NSA_GROUP = NSA_HEADS // NSA_KV_HEADS
NSA_WIDTH = NSA_HEADS * HEAD_DIM
KV_WIDTH = NSA_KV_HEADS * HEAD_DIM
CMP_LEN = 32
CMP_STRIDE = 16
SLC_LEN = 64
SLC_TOPK = 16
WIN_LEN = 512
CONV_K = 3

FOX_HEADS = 24
FOX_WIDTH = FOX_HEADS * HEAD_DIM
POOL_WINDOWS = (2, 4, 8, 16)
POOL_GROUP = 256
POOL_HALO = 16
CONV_HALO = 8

LANES = 128
MXU_WIDTH = 256
VMEM_LIMIT_BYTES = 56 * 1024 * 1024
BF16 = jnp.bfloat16
F32 = jnp.float32
NEG_INF = float("-inf")
LOG2E = 1.4426950408889634
Q_SCALE = ATTN_SCALE * LOG2E
SEL_MASK = -1e30
KV_UNROLL = 8
FOX_PRUNE_BITS = 100.0
CMP_WIDTH_VARIANTS = 4
CMP_Q_TILE = 256


def _params(sem):
    return pltpu.CompilerParams(dimension_semantics=sem, vmem_limit_bytes=VMEM_LIMIT_BYTES)


def _tile(dim, pref):
    if dim <= pref:
        return dim
    t = pref
    while dim % t:
        t //= 2
    return t


def _rmsnorm_kernel(x_ref, g_ref, o_ref):
    x = x_ref[...]
    ms = jnp.mean(x * x, axis=-1, keepdims=True)
    o_ref[...] = (x * lax.rsqrt(ms + NORM_EPS) * g_ref[...]).astype(o_ref.dtype)


def rmsnorm_bf16(x, g):
    s, d = x.shape
    tm = _tile(s, 256)
    return pl.pallas_call(
        _rmsnorm_kernel,
        grid=(s // tm,),
        in_specs=[pl.BlockSpec((tm, d), lambda i: (i, 0)), pl.BlockSpec((1, d), lambda i: (0, 0))],
        out_specs=pl.BlockSpec((tm, d), lambda i: (i, 0)),
        out_shape=jax.ShapeDtypeStruct((s, d), BF16),
        compiler_params=_params(("parallel",)),
        name="rmsnorm",
    )(x, g.reshape(1, d))


def _mm_kernel(*refs, nk, n_extra, epilogue):
    a_ref, b_ref = refs[0], refs[1]
    extra = refs[2:2 + n_extra]
    o_ref = refs[2 + n_extra]
    part = jnp.dot(a_ref[...], b_ref[...].astype(BF16), preferred_element_type=F32)
    if nk == 1:
        o_ref[...] = epilogue(part, *extra).astype(o_ref.dtype)
        return
    acc_ref = refs[3 + n_extra]
    k = pl.program_id(2)

    @pl.when(k == 0)
    def _():
        acc_ref[...] = part

    @pl.when(k > 0)
    def _():
        acc_ref[...] += part

    @pl.when(k == nk - 1)
    def _():
        o_ref[...] = epilogue(acc_ref[...], *extra).astype(o_ref.dtype)


class Weight(NamedTuple):
    arr: jax.Array
    layer: Optional[int] = None
    col0: int = 0
    n: Optional[int] = None

    @property
    def k(self):
        return self.arr.shape[-2]

    @property
    def cols(self):
        return self.n if self.n is not None else self.arr.shape[-1] - self.col0

    def spec(self, tk, tn, k_of, j_of):
        assert self.col0 % tn == 0
        j0 = self.col0 // tn
        if self.layer is None:
            return pl.BlockSpec((tk, tn), lambda *g: (k_of(*g), j0 + j_of(*g)))
        return pl.BlockSpec((None, tk, tn), lambda *g: (self.layer, k_of(*g), j0 + j_of(*g)))


def _lhs_spec(block, index_map, single_buffer):
    if single_buffer:
        return pl.BlockSpec(block, index_map, pipeline_mode=pl.Buffered(1))
    return pl.BlockSpec(block, index_map)


def matmul(a, b, *, out_dtype, epilogue=None, extras=(), tm=1024, tn=512, tk=4096, name="matmul"):
    if not isinstance(b, Weight):
        b = Weight(b)
    m, kd = a.shape
    n = b.cols
    tm, tn, tk = _tile(m, tm), _tile(n, tn), _tile(kd, tk)
    nk = kd // tk
    if epilogue is None:
        epilogue = lambda acc: acc
    in_specs = [pl.BlockSpec((tm, tk), lambda i, j, k: (i, k)), b.spec(tk, tn, lambda i, j, k: k, lambda i, j, k: j)]
    in_specs += [pl.BlockSpec(bs, im) for _, bs, im in extras]
    scratch = [pltpu.VMEM((tm, tn), F32)] if nk > 1 else []
    return pl.pallas_call(
        functools.partial(_mm_kernel, nk=nk, n_extra=len(extras), epilogue=epilogue),
        grid=(m // tm, n // tn, nk),
        in_specs=in_specs,
        out_specs=pl.BlockSpec((tm, tn), lambda i, j, k: (i, j)),
        out_shape=jax.ShapeDtypeStruct((m, n), out_dtype),
        scratch_shapes=scratch,
        compiler_params=_params(("parallel", "parallel", "arbitrary")),
        name=name,
    )(a, b.arr, *[e[0] for e in extras])


def _row_extra(arr, tn):
    return (arr, (1, tn), lambda i, j, k: (0, j))


def _tile_extra(arr, tm, tn):
    return (arr, (tm, tn), lambda i, j, k: (i, j))


def _head_proj_kernel(*refs, norm, rope, scale):
    a_ref, b_ref, o_ref = refs[0], refs[1], refs[-1]
    extra = list(refs[2:-1])
    g_ref = extra.pop(0) if norm else None
    if rope:
        cos_ref, sa_ref, sb_ref = extra
    a = a_ref[...]
    tn = b_ref.shape[1]
    piece = min(MXU_WIDTH, tn)
    for p0 in range(0, tn, piece):
        acc = jnp.dot(a, b_ref[:, p0:p0 + piece], preferred_element_type=F32)
        for c0 in range(0, piece, HEAD_DIM):
            blk = acc[:, c0:c0 + HEAD_DIM]
            cols = slice(p0 + c0, p0 + c0 + HEAD_DIM)
            if norm:
                ms = jnp.mean(blk * blk, axis=-1, keepdims=True)
                blk = blk * lax.rsqrt(ms + NORM_EPS) * g_ref[:, cols]
            if rope:
                blk = (blk * cos_ref[...] + pltpu.roll(blk, HEAD_DIM - ROPE_HALF, 1) * sa_ref[...]
                       + pltpu.roll(blk, ROPE_HALF, 1) * sb_ref[...])
            if scale != 1.0:
                blk = blk * scale
            o_ref[:, cols] = blk.astype(o_ref.dtype)


def head_proj(h, w, *, gain=None, rope=None, scale=1.0, name):
    m, kd = h.shape
    n = w.cols
    tm, tn = _tile(m, 1024), _tile(n, 1024)
    arrays, specs = [], []
    if gain is not None:
        arrays.append(gain.reshape(1, n))
        specs.append(pl.BlockSpec((1, tn), lambda i, j: (0, j)))
    if rope is not None:
        arrays += list(rope)
        specs += [pl.BlockSpec((tm, HEAD_DIM), lambda i, j: (i, 0))] * len(rope)
    return pl.pallas_call(
        functools.partial(_head_proj_kernel, norm=gain is not None, rope=rope is not None, scale=scale),
        grid=(m // tm, n // tn),
        in_specs=[pl.BlockSpec((tm, kd), lambda i, j: (i, 0)), w.spec(kd, tn, lambda i, j: 0, lambda i, j: j)] + specs,
        out_specs=pl.BlockSpec((tm, tn), lambda i, j: (i, j)),
        out_shape=jax.ShapeDtypeStruct((m, n), BF16),
        compiler_params=_params(("parallel", "parallel")),
        name=name,
    )(h, w.arr, *arrays)


def _rope_table_kernel(pos_ref, freq_ref, cos_ref, sa_ref, sb_ref):
    ang = pos_ref[...] * freq_ref[...]
    lane = lax.broadcasted_iota(jnp.int32, ang.shape, 1)
    c, s = jnp.cos(ang), jnp.sin(ang)
    cos_ref[...] = jnp.where(lane < ROPE_DIM, c, 1.0)
    sa_ref[...] = jnp.where(lane < ROPE_HALF, -s, 0.0)
    sb_ref[...] = jnp.where((lane >= ROPE_HALF) & (lane < ROPE_DIM), s, 0.0)


def rope_tables(positions):
    s = positions.shape[0]
    inv_freq = ROPE_THETA ** (-jnp.arange(ROPE_HALF, dtype=F32) / ROPE_HALF)
    freq_row = jnp.zeros((HEAD_DIM,), F32).at[:ROPE_DIM].set(jnp.concatenate([inv_freq, inv_freq]))
    pos_rep = jnp.broadcast_to(positions.astype(F32)[:, None], (s, HEAD_DIM))
    tm = _tile(s, 1024)
    spec = pl.BlockSpec((tm, HEAD_DIM), lambda i: (i, 0))
    shp = jax.ShapeDtypeStruct((s, HEAD_DIM), F32)
    return pl.pallas_call(
        _rope_table_kernel,
        grid=(s // tm,),
        in_specs=[spec, pl.BlockSpec((1, HEAD_DIM), lambda i: (0, 0))],
        out_specs=[spec, spec, spec],
        out_shape=[shp, shp, shp],
        compiler_params=_params(("parallel",)),
        name="rope_tables",
    )(pos_rep, freq_row.reshape(1, HEAD_DIM))


def _gelu_tanh(x):
    return 0.5 * x * (1.0 + jnp.tanh(np.sqrt(2.0 / np.pi).astype(np.float32) * (x + 0.044715 * (x * x * x))))


def _compress_kernel(a_ref, w_ref, pe_ref, w2_ref, g_ref, o_ref, acc_ref, bias_ref, *, norm):
    l = pl.program_id(0)
    nl = pl.num_programs(0)
    a = a_ref[...]
    w = w_ref[0]
    pe_part = jnp.dot(pe_ref[0], w, preferred_element_type=F32)

    @pl.when(l == 0)
    def _():
        bias_ref[...] = pe_part
        for g in range(NSA_KV_HEADS):
            acc_ref[g] = jnp.dot(a[:, g * HEAD_DIM:(g + 1) * HEAD_DIM], w, preferred_element_type=F32)

    @pl.when(l > 0)
    def _():
        bias_ref[...] += pe_part
        for g in range(NSA_KV_HEADS):
            acc_ref[g] += jnp.dot(a[:, g * HEAD_DIM:(g + 1) * HEAD_DIM], w, preferred_element_type=F32)

    @pl.when(l == nl - 1)
    def _():
        n_chunk = a.shape[0]
        bias = bias_ref[0:1, :HEAD_DIM] + bias_ref[1:2, HEAD_DIM:]
        for g in range(NSA_KV_HEADS):
            p = acc_ref[g]
            hid = p[:, :HEAD_DIM] + pltpu.roll(p[:, HEAD_DIM:], n_chunk - 1, 0) + bias
            out = jnp.dot(_gelu_tanh(hid).astype(BF16), w2_ref[...], preferred_element_type=F32)
            if norm:
                ms = jnp.mean(out * out, axis=-1, keepdims=True)
                out = out * lax.rsqrt(ms + NORM_EPS) * g_ref[...]
            o_ref[:, g * HEAD_DIM:(g + 1) * HEAD_DIM] = out.astype(o_ref.dtype)


def compress(kv, pe, w1, w2, gain):
    s = kv.shape[0]
    n_chunk = s // CMP_STRIDE
    a = kv.reshape(n_chunk, CMP_STRIDE * KV_WIDTH)
    r = CMP_LEN // CMP_STRIDE
    w1r = w1.reshape(r, CMP_STRIDE, HEAD_DIM, HEAD_DIM)
    wcat = jnp.concatenate([w1r[0], w1r[1]], axis=-1).astype(BF16)
    pe_r = pe.reshape(r, CMP_STRIDE, HEAD_DIM).transpose(1, 0, 2)
    pe_l = jnp.zeros((CMP_STRIDE, 8, HEAD_DIM), F32).at[:, :r].set(pe_r).astype(BF16)
    norm = gain is not None
    g = (gain if norm else jnp.ones((HEAD_DIM,), F32)).reshape(1, HEAD_DIM)
    return pl.pallas_call(
        functools.partial(_compress_kernel, norm=norm),
        grid=(CMP_STRIDE,),
        in_specs=[
            pl.BlockSpec((n_chunk, KV_WIDTH), lambda l: (0, l)),
            pl.BlockSpec((1, HEAD_DIM, 2 * HEAD_DIM), lambda l: (l, 0, 0)),
            pl.BlockSpec((1, 8, HEAD_DIM), lambda l: (l, 0, 0)),
            pl.BlockSpec((HEAD_DIM, HEAD_DIM), lambda l: (0, 0)),
            pl.BlockSpec((1, HEAD_DIM), lambda l: (0, 0)),
        ],
        out_specs=pl.BlockSpec((n_chunk, KV_WIDTH), lambda l: (0, 0)),
        out_shape=jax.ShapeDtypeStruct((n_chunk, KV_WIDTH), BF16),
        scratch_shapes=[pltpu.VMEM((NSA_KV_HEADS, n_chunk, 2 * HEAD_DIM), F32), pltpu.VMEM((8, 2 * HEAD_DIM), F32)],
        compiler_params=_params(("arbitrary",)),
        name="nsa_compress",
    )(a, wcat, pe_l, w2.astype(BF16), g)


def _stack_heads(qb):
    return jnp.concatenate([qb[:, j * HEAD_DIM:(j + 1) * HEAD_DIM] for j in range(NSA_GROUP)], axis=0)


def _cmp_body(i, q_ref, k_ref, v_ref, ov_ref, o_ref, sel_ref, width, nbw):
    tq = q_ref.shape[0]
    nb = sel_ref.shape[-1]
    q4 = _stack_heads(q_ref[...])
    s = lax.dot_general(q4, k_ref[:width, :], (((1,), (1,)), ((), ())), preferred_element_type=F32)
    rows = lax.broadcasted_iota(jnp.int32, (NSA_GROUP * tq, width), 0)
    cols = lax.broadcasted_iota(jnp.int32, (NSA_GROUP * tq, width), 1)
    tpos = i * tq + rows % tq
    s = jnp.where(cols * CMP_STRIDE + (CMP_LEN - 1) <= tpos, s, NEG_INF)
    m = jnp.max(s, axis=-1, keepdims=True)
    m = jnp.where(m == NEG_INF, 0.0, m)
    e = jnp.exp2(s - m)
    p = e * (1.0 / jnp.maximum(jnp.sum(e, axis=-1, keepdims=True), 1e-30))
    o = jnp.dot(p.astype(BF16), v_ref[:width, :], preferred_element_type=F32)
    for j in range(NSA_GROUP):
        o_ref[:, j * HEAD_DIM:(j + 1) * HEAD_DIM] = o[j * tq:(j + 1) * tq]

    psum = p[0:tq] + p[tq:2 * tq] + p[2 * tq:3 * tq] + p[3 * tq:4 * tq]
    p_hi = psum.astype(BF16)
    p_lo = (psum - p_hi.astype(F32)).astype(BF16)
    ov = ov_ref[:width, :nbw]
    imp = jnp.dot(p_hi, ov, preferred_element_type=F32) + jnp.dot(p_lo, ov, preferred_element_type=F32)

    blk = lax.broadcasted_iota(jnp.int32, (tq, nbw), 1)
    t = i * tq + lax.broadcasted_iota(jnp.int32, (tq, nbw), 0)
    cur = t // SLC_LEN
    forced = (blk == 0) | (blk == cur) | (blk == cur - 1)
    valid = blk * SLC_LEN <= t
    work = jnp.where(forced, jnp.inf, jnp.where(valid, imp, NEG_INF))
    sel = jnp.zeros((tq, nbw), F32)
    n_valid = jnp.sum(jnp.where(work > NEG_INF, 1.0, 0.0), axis=-1, keepdims=True)
    for r in range(min(SLC_TOPK, nb)):
        first = jnp.argmax(work, axis=-1, keepdims=True)
        pick = (blk == first) & (n_valid > r)
        sel = jnp.where(pick, 1.0, sel)
        work = jnp.where(pick, NEG_INF, work)
    bias = jnp.where(sel > 0.5, 0.0, SEL_MASK)
    if nbw < nb:
        bias = jnp.concatenate([bias, jnp.full((tq, nb - nbw), SEL_MASK, F32)], axis=1)
    sel_ref[...] = bias.astype(sel_ref.dtype)


def _cmp_kernel(q_ref, k_ref, v_ref, ov_ref, o_ref, sel_ref, *, col_chunk):
    i = pl.program_id(1)
    tq = q_ref.shape[0]
    n_chunk = k_ref.shape[0]
    nb = sel_ref.shape[-1]
    n_var = n_chunk // col_chunk
    need = (i * tq + tq - CMP_LEN) // CMP_STRIDE + 1
    variant = jnp.minimum((need - 1) // col_chunk, n_var - 1)
    for v in range(n_var):
        width = (v + 1) * col_chunk
        i_max = (CMP_STRIDE * width + CMP_LEN - 1) // tq - 1
        blocks = ((i_max + 1) * tq - 1) // SLC_LEN + 1
        nbw = min(nb, -(-blocks // LANES) * LANES)

        @pl.when(variant == v)
        def _(width=width, nbw=nbw):
            _cmp_body(i, q_ref, k_ref, v_ref, ov_ref, o_ref, sel_ref, width, nbw)


def nsa_cmp_and_select(q, k_cmp, v_cmp):
    s = q.shape[0]
    n_chunk = k_cmp.shape[0]
    nb = s // SLC_LEN
    tq = _tile(s, CMP_Q_TILE)
    c_start = np.arange(n_chunk) * CMP_STRIDE
    s_start = np.arange(nb) * SLC_LEN
    overlap = np.clip(np.minimum(c_start[:, None] + CMP_LEN, s_start[None, :] + SLC_LEN)
                      - np.maximum(c_start[:, None], s_start[None, :]), 0, None).astype(np.float32) / CMP_LEN
    gw = NSA_GROUP * HEAD_DIM
    col_chunk = max(LANES, n_chunk // CMP_WIDTH_VARIANTS)
    return pl.pallas_call(
        functools.partial(_cmp_kernel, col_chunk=col_chunk),
        grid=(NSA_KV_HEADS, s // tq),
        in_specs=[
            pl.BlockSpec((tq, gw), lambda g, i: (i, g)),
            pl.BlockSpec((n_chunk, HEAD_DIM), lambda g, i: (0, g)),
            pl.BlockSpec((n_chunk, HEAD_DIM), lambda g, i: (0, g)),
            pl.BlockSpec((n_chunk, nb), lambda g, i: (0, 0)),
        ],
        out_specs=[
            pl.BlockSpec((tq, gw), lambda g, i: (i, g)),
            pl.BlockSpec((None, tq, nb), lambda g, i: (g, i, 0)),
        ],
        out_shape=[jax.ShapeDtypeStruct((s, NSA_WIDTH), F32), jax.ShapeDtypeStruct((NSA_KV_HEADS, s, nb), BF16)],
        compiler_params=_params(("parallel", "parallel")),
        name="nsa_cmp_select",
    )(q, k_cmp, v_cmp, jnp.asarray(overlap, BF16))


def _flash_tile(q2, k2, v, m_ref, acc_ref, mask=None):
    tk = k2.shape[0]
    s = lax.dot_general(q2, k2, (((1,), (1,)), ((), ())), preferred_element_type=F32)
    if mask is not None:
        s = jnp.where(mask, s, NEG_INF)
    m_old = m_ref[...]
    m_new = jnp.maximum(m_old, jnp.max(s, axis=-1, keepdims=True))
    alpha = jnp.exp2(m_old - m_new)
    p = jnp.exp2(s - jnp.concatenate([m_new] * (tk // LANES), axis=1))
    v2 = jnp.concatenate([v, jnp.ones((tk, LANES), v.dtype)], axis=1)
    acc_ref[...] = jnp.concatenate([alpha, alpha], axis=1) * acc_ref[...] + jnp.dot(
        p.astype(BF16), v2, preferred_element_type=F32)
    m_ref[...] = m_new


def _init_flash(m_ref, acc_ref):
    m_ref[...] = jnp.full(m_ref.shape, NEG_INF, F32)
    acc_ref[...] = jnp.zeros(acc_ref.shape, F32)


def _flash_result(acc_ref):
    acc = acc_ref[...]
    return acc[:, :HEAD_DIM] * (1.0 / acc[:, HEAD_DIM:])


def _for_each_tile(n, tile_fn):
    def body(jj, carry):
        for u in range(KV_UNROLL):
            tile_fn(jj * KV_UNROLL + u)
        return carry

    lax.fori_loop(0, n // KV_UNROLL, body, 0)
    base = (n // KV_UNROLL) * KV_UNROLL
    g = KV_UNROLL // 2
    while g >= 1:
        take = (n & g) != 0

        @pl.when(take)
        def _(base=base, g=g):
            for u in range(g):
                tile_fn(base + u)

        base = base + jnp.where(take, g, 0)
        g //= 2


def _slc_kernel(q_ref, bias_ref, k_ref, kx_ref, v_ref, o_ref, q2_ref, m_ref, acc_ref, *, tk):
    i = pl.program_id(1)
    tq = q_ref.shape[0]
    cw = kx_ref.shape[1]
    _init_flash(m_ref, acc_ref)
    q4 = _stack_heads(q_ref[...])
    for c in range(q2_ref.shape[0]):
        bias = bias_ref[:, c * cw:(c + 1) * cw]
        q2_ref[c] = jnp.concatenate([q4, jnp.concatenate([bias] * NSA_GROUP, axis=0)], axis=1)

    def tile(j, masked):
        off = pl.multiple_of(j * tk, tk)
        k2 = jnp.concatenate([k_ref[pl.ds(off, tk), :], kx_ref[pl.ds(off, tk), :]], axis=1)
        q2 = q2_ref[(j * tk) // (SLC_LEN * cw)]
        mask = None
        if masked:
            rows = lax.broadcasted_iota(jnp.int32, (NSA_GROUP * tq, tk), 0)
            cols = lax.broadcasted_iota(jnp.int32, (NSA_GROUP * tq, tk), 1)
            mask = off + cols <= i * tq + rows % tq
        _flash_tile(q2, k2, v_ref[pl.ds(off, tk), :], m_ref, acc_ref, mask)

    last = (i * tq) // tk
    _for_each_tile(last, lambda j: tile(j, False))
    tile(last, True)
    o = _flash_result(acc_ref)
    for h in range(NSA_GROUP):
        o_ref[:, h * HEAD_DIM:(h + 1) * HEAD_DIM] = o[h * tq:(h + 1) * tq]


def nsa_selected(q, k_slc, v_slc, sel_bias):
    s = q.shape[0]
    nb = s // SLC_LEN
    cw = min(nb, LANES)
    tq, tk = Q_BLOCK, _tile(s, 512)
    key_blk = jnp.arange(s, dtype=jnp.int32)[:, None] // SLC_LEN
    key_onehot = (key_blk % cw == jnp.arange(cw, dtype=jnp.int32)[None, :]).astype(BF16)
    gw = NSA_GROUP * HEAD_DIM
    rows = NSA_GROUP * tq
    return pl.pallas_call(
        functools.partial(_slc_kernel, tk=tk),
        grid=(NSA_KV_HEADS, s // tq),
        in_specs=[
            pl.BlockSpec((tq, gw), lambda g, i: (i, g)),
            pl.BlockSpec((None, tq, nb), lambda g, i: (g, i, 0)),
            pl.BlockSpec((s, HEAD_DIM), lambda g, i: (0, g)),
            pl.BlockSpec((s, cw), lambda g, i: (0, 0)),
            pl.BlockSpec((s, HEAD_DIM), lambda g, i: (0, g)),
        ],
        out_specs=pl.BlockSpec((tq, gw), lambda g, i: (i, g)),
        out_shape=jax.ShapeDtypeStruct((s, NSA_WIDTH), F32),
        scratch_shapes=[pltpu.VMEM((nb // cw, rows, HEAD_DIM + cw), BF16), pltpu.VMEM((rows, LANES), F32),
                        pltpu.VMEM((rows, 2 * HEAD_DIM), F32)],
        compiler_params=_params(("parallel", "arbitrary")),
        name="nsa_selected",
    )(q, sel_bias, k_slc, key_onehot, v_slc)


WIN_TILES = WIN_LEN // Q_BLOCK + 1


def _win_kernel(*refs):
    q_ref = refs[0]
    k_refs = refs[1:1 + WIN_TILES]
    v_refs = refs[1 + WIN_TILES:1 + 2 * WIN_TILES]
    ocmp_ref, oslc_ref, gate_ref, o_ref = refs[1 + 2 * WIN_TILES:]
    i = pl.program_id(0)
    tq = q_ref.shape[0]
    q4 = _stack_heads(q_ref[...])
    kcat = jnp.concatenate([r[...] for r in k_refs], axis=0)
    vcat = jnp.concatenate([r[...] for r in v_refs], axis=0)
    s = lax.dot_general(q4, kcat, (((1,), (1,)), ((), ())), preferred_element_type=F32)
    rows = lax.broadcasted_iota(jnp.int32, s.shape, 0)
    cols = lax.broadcasted_iota(jnp.int32, s.shape, 1)
    tpos = i * tq + rows % tq
    wpos = (i - (WIN_TILES - 1)) * tq + cols
    ok = (wpos <= tpos) & (wpos > tpos - WIN_LEN) & (wpos >= 0)
    s = jnp.where(ok, s, NEG_INF)
    m = jnp.max(s, axis=-1, keepdims=True)
    e = jnp.exp2(s - m)
    p = e * (1.0 / jnp.sum(e, axis=-1, keepdims=True))
    o_win = jnp.dot(p.astype(BF16), vcat, preferred_element_type=F32)
    gates = gate_ref[...]
    for h in range(NSA_GROUP):
        sl = slice(h * HEAD_DIM, (h + 1) * HEAD_DIM)
        o = (gates[:, 3 * h:3 * h + 1] * ocmp_ref[:, sl] + gates[:, 3 * h + 1:3 * h + 2] * oslc_ref[:, sl]
             + gates[:, 3 * h + 2:3 * h + 3] * o_win[h * tq:(h + 1) * tq])
        o_ref[:, sl] = o.astype(o_ref.dtype)


def nsa_window_combine(q, k_win, v_win, o_cmp, o_slc, gates, out_width):
    s = q.shape[0]
    tq = Q_BLOCK
    gw = NSA_GROUP * HEAD_DIM
    back = WIN_TILES - 1
    kv_specs = [pl.BlockSpec((tq, HEAD_DIM), lambda i, g, d=d: (jnp.maximum(i - back + d, 0), g))
                for d in range(WIN_TILES)]
    blk = pl.BlockSpec((tq, gw), lambda i, g: (i, g))
    return pl.pallas_call(
        _win_kernel,
        grid=(s // tq, NSA_KV_HEADS),
        in_specs=[blk] + kv_specs + kv_specs + [blk, blk, pl.BlockSpec((None, tq, 3 * NSA_GROUP), lambda i, g: (g, i, 0))],
        out_specs=blk,
        out_shape=jax.ShapeDtypeStruct((s, out_width), BF16),
        compiler_params=_params(("parallel", "parallel")),
        name="nsa_window_combine",
    )(q, *([k_win] * WIN_TILES), *([v_win] * WIN_TILES), o_cmp, o_slc, gates)


def _conv_kernel(b_ref, c_ref, h_ref, ch_ref, hh_ref, w_ref, dst_ref, o_ref):
    del dst_ref
    i = pl.program_id(0)
    u = c_ref[...] * h_ref[...]
    halo = jnp.where(i > 0, ch_ref[...] * hh_ref[...], 0.0)
    x = jnp.concatenate([halo, u], axis=0)
    w = w_ref[...]
    y = (w[2:3] * x + w[1:2] * pltpu.roll(x, 1, 0) + w[0:1] * pltpu.roll(x, 2, 0))[CONV_HALO:]
    o_ref[...] = (b_ref[...] * y).astype(o_ref.dtype)


def short_conv(z, conv_w, dst):
    s = z.shape[0]
    cw = conv_w.shape[1]
    tm, tc = _tile(s, 512), _tile(cw, 512)
    nc = cw // tc
    hb = tm // CONV_HALO
    col0 = dst.shape[1] - cw
    assert col0 % tc == 0
    halo = lambda off: pl.BlockSpec((CONV_HALO, tc), lambda i, j: (jnp.maximum(i * hb - 1, 0), off * nc + j))
    main = lambda off: pl.BlockSpec((tm, tc), lambda i, j: (i, off * nc + j))
    w8 = jnp.zeros((8, cw), F32).at[:CONV_K].set(conv_w)
    return pl.pallas_call(
        _conv_kernel,
        grid=(s // tm, nc),
        in_specs=[main(0), main(1), main(2), halo(1), halo(2), pl.BlockSpec((8, tc), lambda i, j: (0, j)),
                  pl.BlockSpec(memory_space=pl.ANY)],
        out_specs=pl.BlockSpec((tm, tc), lambda i, j: (i, col0 // tc + j)),
        out_shape=jax.ShapeDtypeStruct(dst.shape, dst.dtype),
        input_output_aliases={6: 0},
        compiler_params=_params(("parallel", "parallel")),
        name="short_conv",
    )(z, z, z, z, z, w8, dst)


def _split3(x):
    hi = x.astype(BF16)
    r1 = x - hi.astype(F32)
    mid = r1.astype(BF16)
    lo = (r1 - mid.astype(F32)).astype(BF16)
    return hi, mid, lo


def _cumsum_kernel(x_ref, hi_ref, mid_ref, lo_ref, carry_ref):
    @pl.when(pl.program_id(0) == 0)
    def _():
        carry_ref[...] = jnp.zeros(carry_ref.shape, F32)

    x = x_ref[...]
    t = x.shape[0]
    tri = (lax.broadcasted_iota(jnp.int32, (t, t), 0) >= lax.broadcasted_iota(jnp.int32, (t, t), 1)).astype(BF16)
    c = sum(jnp.dot(tri, part, preferred_element_type=F32) for part in _split3(x)) + carry_ref[0:1]
    carry_ref[...] = jnp.broadcast_to(c[t - 1:t], carry_ref.shape)
    hi_ref[...], mid_ref[...], lo_ref[...] = _split3(c * LOG2E)


def cumsum_log2_split(x):
    s, w = x.shape
    t = _tile(s, 512)
    spec = pl.BlockSpec((t, w), lambda i: (i, 0))
    shp = jax.ShapeDtypeStruct((s, w), BF16)
    return pl.pallas_call(
        _cumsum_kernel,
        grid=(s // t,),
        in_specs=[spec],
        out_specs=[spec, spec, spec],
        out_shape=[shp, shp, shp],
        scratch_shapes=[pltpu.VMEM((8, w), F32)],
        compiler_params=_params(("arbitrary",)),
        name="cumsum",
    )(x)


def _fox_kernel(q_ref, qx_ref, k_ref, kx_ref, v_ref, cq_ref, ck_ref, o_ref, m_ref, acc_ref, kn_ref, *, t):
    i = pl.program_id(1)
    _init_flash(m_ref, acc_ref)
    q = q_ref[...]
    q2 = jnp.concatenate([q, qx_ref[...]], axis=1)

    @pl.when(i == 0)
    def _():
        def body(j, mx):
            kk = k_ref[pl.ds(pl.multiple_of(j * t, t), t), :].astype(F32)
            return jnp.maximum(mx, jnp.max(jnp.sum(kk * kk, axis=1, keepdims=True), axis=0, keepdims=True))
        kn_ref[...] = jnp.broadcast_to(lax.fori_loop(0, k_ref.shape[0] // t, body, jnp.zeros((1, 1), F32)), kn_ref.shape)

    qq = q.astype(F32)
    bound = jnp.max(jnp.sqrt(jnp.sum(qq * qq, axis=1, keepdims=True) * kn_ref[0:1, 0:1]), axis=0, keepdims=True)
    gap = cq_ref[pl.ds(i, 1), :] - ck_ref[...]
    lane = lax.broadcasted_iota(jnp.int32, gap.shape, 1)
    keep = (gap >= -(2.0 * bound + FOX_PRUNE_BITS)) & (lane < i)
    n_keep = jnp.sum(jnp.where(keep, 1, 0))
    first = i - n_keep

    def tile(j, masked):
        off = pl.multiple_of(j * t, t)
        k2 = jnp.concatenate([k_ref[pl.ds(off, t), :], kx_ref[pl.ds(off, t), :]], axis=1)
        mask = None
        if masked:
            mask = lax.broadcasted_iota(jnp.int32, (t, t), 1) <= lax.broadcasted_iota(jnp.int32, (t, t), 0)
        _flash_tile(q2, k2, v_ref[pl.ds(off, t), :], m_ref, acc_ref, mask)

    _for_each_tile(n_keep, lambda jj: tile(first + jj, False))
    tile(i, True)
    o_ref[...] = _flash_result(acc_ref).astype(o_ref.dtype)


def fox_attention(q, k, v, c_parts, out_width):
    s = q.shape[0]
    t = _tile(s, 512)
    parts = jnp.stack([part[:, :FOX_HEADS] for part in c_parts], axis=-1)
    ones = jnp.ones_like(parts)
    widen = lambda a: jnp.pad(a, ((0, 0), (0, 0), (0, HEAD_DIM - a.shape[-1]))).reshape(s, FOX_WIDTH)
    qx = widen(jnp.concatenate([parts, ones], axis=-1))
    kx = widen(jnp.concatenate([ones, -parts], axis=-1))
    nt = s // t
    assert nt <= LANES
    c = sum(part[:, :FOX_HEADS].astype(F32) for part in c_parts)
    cq_first = jnp.broadcast_to(c[0::t].T[:, :, None], (FOX_HEADS, nt, LANES))
    ck_last = jnp.pad(c[t - 1::t].T, ((0, 0), (0, LANES - nt))).reshape(FOX_HEADS, 1, LANES)
    tile_spec = pl.BlockSpec((t, HEAD_DIM), lambda h, i: (i, h))
    full_spec = pl.BlockSpec((s, HEAD_DIM), lambda h, i: (0, h))
    return pl.pallas_call(
        functools.partial(_fox_kernel, t=t),
        grid=(FOX_HEADS, nt),
        in_specs=[tile_spec, tile_spec, full_spec, full_spec, full_spec,
                  pl.BlockSpec((None, nt, LANES), lambda h, i: (h, 0, 0)),
                  pl.BlockSpec((None, 1, LANES), lambda h, i: (h, 0, 0))],
        out_specs=tile_spec,
        out_shape=jax.ShapeDtypeStruct((s, out_width), BF16),
        scratch_shapes=[pltpu.VMEM((t, LANES), F32), pltpu.VMEM((t, 2 * HEAD_DIM), F32), pltpu.VMEM((8, LANES), F32)],
        compiler_params=_params(("parallel", "arbitrary")),
        name="fox_attention",
    )(q, qx, k, kx, v, cq_first, ck_last)


def _pool_kernel(u_ref, halo_ref, w_ref, scale_ref, dst_ref, o_ref):
    del dst_ref
    i = pl.program_id(0)
    tm = u_ref.shape[0]
    u = u_ref[...]
    halo = jnp.where(i > 0, halo_ref[...], 0.0)
    x = jnp.concatenate([halo, u], axis=0)
    t1 = i * tm + lax.broadcasted_iota(jnp.int32, (tm, POOL_GROUP), 0) + 1
    for g, win in enumerate(POOL_WINDOWS):
        sl = slice(g * POOL_GROUP, (g + 1) * POOL_GROUP)
        acc = x[:, sl]
        span = 1
        while span < win:
            acc = acc + pltpu.roll(acc, span, 0)
            span *= 2
        cnt = jnp.minimum(t1, win).astype(F32)
        d = acc[POOL_HALO:] / cnt - u[:, sl]
        y = jnp.dot(d.astype(BF16), w_ref[g], preferred_element_type=F32)
        o_ref[:, sl] = (y * scale_ref[:, sl]).astype(o_ref.dtype)


def multiscale_pool(u, w_pool, scale, dst):
    s, pw = u.shape
    tm = _tile(s, 512)
    hb = tm // POOL_HALO
    col0 = dst.shape[1] - pw
    assert col0 % pw == 0
    return pl.pallas_call(
        _pool_kernel,
        grid=(s // tm,),
        in_specs=[
            pl.BlockSpec((tm, pw), lambda i: (i, 0)),
            pl.BlockSpec((POOL_HALO, pw), lambda i: (jnp.maximum(i * hb - 1, 0), 0)),
            pl.BlockSpec(w_pool.shape, lambda i: (0, 0, 0)),
            pl.BlockSpec((1, pw), lambda i: (0, 0)),
            pl.BlockSpec(memory_space=pl.ANY),
        ],
        out_specs=pl.BlockSpec((tm, pw), lambda i: (i, col0 // pw)),
        out_shape=jax.ShapeDtypeStruct(dst.shape, dst.dtype),
        input_output_aliases={4: 0},
        compiler_params=_params(("parallel",)),
        name="multiscale_pool",
    )(u, u, w_pool.astype(BF16), scale.reshape(1, pw), dst)


def _swiglu_kernel(h_ref, w1_ref, w3_ref, o_ref):
    h = h_ref[...]
    a = jnp.dot(h, w1_ref[...].astype(BF16), preferred_element_type=F32)
    b = jnp.dot(h, w3_ref[...].astype(BF16), preferred_element_type=F32)
    o_ref[...] = (a * jax.nn.sigmoid(a) * b).astype(o_ref.dtype)


def swiglu_up(h, w1, w3):
    m, d = h.shape
    n = w1.cols
    tm, tn = _tile(m, 2048), _tile(n, 512)
    specs = [w.spec(d, tn, lambda i, j: 0, lambda i, j: j) for w in (w1, w3)]
    return pl.pallas_call(
        _swiglu_kernel,
        grid=(m // tm, n // tn),
        in_specs=[_lhs_spec((tm, d), lambda i, j: (i, 0), True)] + specs,
        out_specs=pl.BlockSpec((tm, tn), lambda i, j: (i, j)),
        out_shape=jax.ShapeDtypeStruct((m, n), BF16),
        compiler_params=_params(("parallel", "parallel")),
        name="swiglu_up",
    )(h, w1.arr, w3.arr)


def _resid_epilogue(acc, x_ref):
    return x_ref[...] + acc


def _gate_epilogue(acc, x_ref, p_ref, wp_ref):
    return x_ref[...] + jax.nn.sigmoid(acc) * jnp.dot(p_ref[...], wp_ref[...].astype(BF16),
                                                      preferred_element_type=F32)


def _logsig_epilogue(acc, b_ref):
    y = -(acc + b_ref[...])
    return -(jnp.maximum(y, 0.0) + jnp.log1p(jnp.exp(-jnp.abs(y))))


def _sigmoid_epilogue(acc):
    return jax.nn.sigmoid(acc)


def matmul_resid(a, b, x, *, tk=4096, name):
    tm, tn = _tile(a.shape[0], 1024), _tile(b.cols, 512)
    return matmul(a, b, out_dtype=F32, epilogue=_resid_epilogue, extras=[_tile_extra(x, tm, tn)], tm=tm, tn=tn, tk=tk,
                  name=name)


def _k_tile(k, cap):
    for t in range(cap - cap % LANES, 0, -LANES):
        if k % t == 0:
            return t
    return k


FFN_DOWN_TK_CAP = 5632


def ffn_and_embed(x, p_i, layer, f_norm, w1, w3, w2, e_norm, w_gate, w_proj):
    h = rmsnorm_bf16(x, f_norm)
    u = swiglu_up(h, Weight(w1, layer), Weight(w3, layer))
    x = matmul_resid(u, Weight(w2, layer), x, tk=_k_tile(w2.shape[-2], FFN_DOWN_TK_CAP), name="ffn_down")
    h = rmsnorm_bf16(x, e_norm)
    m, d = x.shape
    tm, tn = _tile(m, 1024), _tile(d, 512)
    pd = p_i.shape[1]
    extras = [_tile_extra(x, tm, tn), (p_i.astype(BF16), (tm, pd), lambda i, j, k: (i, 0)),
              (w_proj, (None, pd, tn), lambda i, j, k: (layer, 0, j))]
    return matmul(h, Weight(w_gate, layer), out_dtype=F32, epilogue=_gate_epilogue, extras=extras, tm=tm, tn=tn,
                  name="embed_gate")


def even_mixer(x, rope, layer, norm_g, w_in, q_norm, k_norm, cmp_pe, cmp_w1, cmp_w2, conv_w, w_out):
    s = x.shape[0]
    cw = conv_w.shape[1]
    h = rmsnorm_bf16(x, norm_g)
    o_kv = NSA_WIDTH
    o_gate = o_kv + 6 * KV_WIDTH
    o_conv = o_gate + 3 * NSA_HEADS
    kv_cols = lambda i: slice(o_kv + i * KV_WIDTH, o_kv + (i + 1) * KV_WIDTH)
    tile_gain = lambda g, n: jnp.tile(g, n // HEAD_DIM)

    q = head_proj(h, Weight(w_in, layer, 0, NSA_WIDTH), gain=tile_gain(q_norm, NSA_WIDTH), rope=rope, scale=Q_SCALE,
                  name="nsa_q_proj")
    k_cmp_in = head_proj(h, Weight(w_in, layer, o_kv, KV_WIDTH), rope=rope, name="nsa_kcmp_proj")
    w_v = jnp.concatenate([w_in[layer, :, kv_cols(i)] for i in (1, 3, 5)], axis=1)
    v_all = matmul(h, w_v, out_dtype=BF16, name="nsa_v_proj")
    v_cmp_in, v_slc, v_win = (v_all[:, i * KV_WIDTH:(i + 1) * KV_WIDTH] for i in range(3))
    w_k = jnp.concatenate([w_in[layer, :, kv_cols(i)] for i in (2, 4)], axis=1)
    g_k = jnp.concatenate([tile_gain(k_norm[1], KV_WIDTH), tile_gain(k_norm[2], KV_WIDTH)])
    k_both = head_proj(h, Weight(w_k), gain=g_k, rope=rope, name="nsa_k_proj")
    k_slc, k_win = k_both[:, :KV_WIDTH], k_both[:, KV_WIDTH:]
    gates = matmul(h, Weight(w_in, layer, o_gate, LANES), out_dtype=F32, epilogue=_sigmoid_epilogue,
                   name="nsa_gate_proj")[:, :3 * NSA_HEADS]
    gates = gates.reshape(s, NSA_KV_HEADS, 3 * NSA_GROUP).transpose(1, 0, 2)
    z_conv = matmul(h, w_in[layer, :, o_conv:], out_dtype=F32, name="conv_proj")

    k_cmp = compress(k_cmp_in, cmp_pe[0], cmp_w1[0], cmp_w2[0], k_norm[0])
    v_cmp = compress(v_cmp_in, cmp_pe[1], cmp_w1[1], cmp_w2[1], None)
    o_cmp, sel = nsa_cmp_and_select(q, k_cmp, v_cmp)
    o_slc = nsa_selected(q, k_slc, v_slc, sel)
    mixed = nsa_window_combine(q, k_win, v_win, o_cmp, o_slc, gates, NSA_WIDTH + cw)
    mixed = short_conv(z_conv, conv_w, mixed)
    return matmul_resid(mixed, Weight(w_out, layer), x, name="even_out_proj")


def odd_mixer(x, layer, norm_g, w_in, f_bias, q_norm, k_norm, pool_w, pool_scale, w_out):
    h = rmsnorm_bf16(x, norm_g)
    o_fgate = 3 * FOX_WIDTH
    o_pool = o_fgate + FOX_HEADS
    pw = pool_scale.shape[0]
    tile_gain = lambda g: jnp.tile(g, FOX_WIDTH // HEAD_DIM)
    q = head_proj(h, Weight(w_in, layer, 0, FOX_WIDTH), gain=tile_gain(q_norm), scale=Q_SCALE, name="fox_q_proj")
    k = head_proj(h, Weight(w_in, layer, FOX_WIDTH, FOX_WIDTH), gain=tile_gain(k_norm), name="fox_k_proj")
    v = matmul(h, Weight(w_in, layer, 2 * FOX_WIDTH, FOX_WIDTH), out_dtype=BF16, name="fox_v_proj")
    b_f = jnp.pad(f_bias, (0, LANES - FOX_HEADS)).reshape(1, LANES)
    log_f = matmul(h, Weight(w_in, layer, o_fgate, LANES), out_dtype=F32, epilogue=_logsig_epilogue,
                   extras=[_row_extra(b_f, LANES)], name="fox_gate_proj")
    u = matmul(h, w_in[layer, :, o_pool:], out_dtype=F32, name="pool_proj")
    mixed = fox_attention(q, k, v, cumsum_log2_split(log_f), FOX_WIDTH + pw)
    mixed = multiscale_pool(u, pool_w, pool_scale, mixed)
    return matmul_resid(mixed, Weight(w_out, layer), x, name="odd_out_proj")


def kernel(x, p, positions, a_norm, a_w_in, a_q_norm, a_k_norm, a_cmp_pe, a_cmp_w1, a_cmp_w2, a_conv_w, a_w_out, b_norm, b_w_in, b_f_bias, b_q_norm, b_k_norm, b_pool_w, b_pool_scale, b_w_out, f_norm, f_w1, f_w3, f_w2, e_norm, e_w_gate, e_w_proj):
    batch, s, d = x.shape
    depth = p.shape[0]
    a_w_in, a_w_out, b_w_in, b_w_out, f_w2, e_w_gate, e_w_proj = (
        w.astype(BF16) for w in (a_w_in, a_w_out, b_w_in, b_w_out, f_w2, e_w_gate, e_w_proj))
    outs = []
    for b in range(batch):
        xb = x[b]
        rope = rope_tables(positions[b])
        for i in range(depth):
            j = i // 2
            if i % 2 == 0:
                xb = even_mixer(xb, rope, j, a_norm[j], a_w_in, a_q_norm[j], a_k_norm[j], a_cmp_pe[j], a_cmp_w1[j],
                                a_cmp_w2[j], a_conv_w[j], a_w_out)
            else:
                xb = odd_mixer(xb, j, b_norm[j], b_w_in, b_f_bias[j], b_q_norm[j], b_k_norm[j], b_pool_w[j],
                               b_pool_scale[j], b_w_out)
            xb = ffn_and_embed(xb, p[i, b], i, f_norm[i], f_w1, f_w3, f_w2, e_norm[i], e_w_gate, e_w_proj)
        outs.append(xb)
    return jnp.stack(outs, axis=0)
```

```python
import functools
from typing import NamedTuple, Optional

import jax
import jax.numpy as jnp
import numpy as np
from jax import lax
from jax.experimental import pallas as pl
from jax.experimental.pallas import tpu as pltpu

HEAD_DIM = 128
ROPE_DIM = HEAD_DIM // 4
ROPE_HALF = ROPE_DIM // 2
ROPE_THETA = 500000.0
NORM_EPS = 1e-6
ATTN_SCALE = HEAD_DIM ** -0.5
Q_BLOCK = 128

NSA_HEADS = 16
NSA_KV_HEADS = 4
NSA_GROUP = NSA_HEADS // NSA_KV_HEADS
NSA_WIDTH = NSA_HEADS * HEAD_DIM
KV_WIDTH = NSA_KV_HEADS * HEAD_DIM
CMP_LEN = 32
CMP_STRIDE = 16
SLC_LEN = 64
SLC_TOPK = 16
WIN_LEN = 512
CONV_K = 3

FOX_HEADS = 24
FOX_WIDTH = FOX_HEADS * HEAD_DIM
POOL_WINDOWS = (2, 4, 8, 16)
POOL_GROUP = 256
POOL_HALO = 16
CONV_HALO = 8

LANES = 128
MXU_WIDTH = 256
VMEM_LIMIT_BYTES = 56 * 1024 * 1024
BF16 = jnp.bfloat16
F32 = jnp.float32
NEG_INF = float("-inf")
LOG2E = 1.4426950408889634
Q_SCALE = ATTN_SCALE * LOG2E
SEL_MASK = -1e30
KV_UNROLL = 8
FOX_PRUNE_BITS = 100.0
CMP_WIDTH_VARIANTS = 4
CMP_Q_TILE = 256


def _params(sem):
    return pltpu.CompilerParams(dimension_semantics=sem, vmem_limit_bytes=VMEM_LIMIT_BYTES)


def _tile(dim, pref):
    if dim <= pref:
        return dim
    t = pref
    while dim % t:
        t //= 2
    return t


def _rmsnorm_kernel(x_ref, g_ref, o_ref):
    x = x_ref[...]
    ms = jnp.mean(x * x, axis=-1, keepdims=True)
    o_ref[...] = (x * lax.rsqrt(ms + NORM_EPS) * g_ref[...]).astype(o_ref.dtype)


def rmsnorm_bf16(x, g):
    s, d = x.shape
    tm = _tile(s, 256)
    return pl.pallas_call(
        _rmsnorm_kernel,
        grid=(s // tm,),
        in_specs=[pl.BlockSpec((tm, d), lambda i: (i, 0)), pl.BlockSpec((1, d), lambda i: (0, 0))],
        out_specs=pl.BlockSpec((tm, d), lambda i: (i, 0)),
        out_shape=jax.ShapeDtypeStruct((s, d), BF16),
        compiler_params=_params(("parallel",)),
        name="rmsnorm",
    )(x, g.reshape(1, d))


def _mm_kernel(*refs, nk, n_extra, epilogue):
    a_ref, b_ref = refs[0], refs[1]
    extra = refs[2:2 + n_extra]
    o_ref = refs[2 + n_extra]
    part = jnp.dot(a_ref[...], b_ref[...].astype(BF16), preferred_element_type=F32)
    if nk == 1:
        o_ref[...] = epilogue(part, *extra).astype(o_ref.dtype)
        return
    acc_ref = refs[3 + n_extra]
    k = pl.program_id(2)

    @pl.when(k == 0)
    def _():
        acc_ref[...] = part

    @pl.when(k > 0)
    def _():
        acc_ref[...] += part

    @pl.when(k == nk - 1)
    def _():
        o_ref[...] = epilogue(acc_ref[...], *extra).astype(o_ref.dtype)


class Weight(NamedTuple):
    arr: jax.Array
    layer: Optional[int] = None
    col0: int = 0
    n: Optional[int] = None

    @property
    def k(self):
        return self.arr.shape[-2]

    @property
    def cols(self):
        return self.n if self.n is not None else self.arr.shape[-1] - self.col0

    def spec(self, tk, tn, k_of, j_of):
        assert self.col0 % tn == 0
        j0 = self.col0 // tn
        if self.layer is None:
            return pl.BlockSpec((tk, tn), lambda *g: (k_of(*g), j0 + j_of(*g)))
        return pl.BlockSpec((None, tk, tn), lambda *g: (self.layer, k_of(*g), j0 + j_of(*g)))


def _lhs_spec(block, index_map, single_buffer):
    if single_buffer:
        return pl.BlockSpec(block, index_map, pipeline_mode=pl.Buffered(1))
    return pl.BlockSpec(block, index_map)


def matmul(a, b, *, out_dtype, epilogue=None, extras=(), tm=1024, tn=512, tk=4096, name="matmul"):
    if not isinstance(b, Weight):
        b = Weight(b)
    m, kd = a.shape
    n = b.cols
    tm, tn, tk = _tile(m, tm), _tile(n, tn), _tile(kd, tk)
    nk = kd // tk
    if epilogue is None:
        epilogue = lambda acc: acc
    in_specs = [pl.BlockSpec((tm, tk), lambda i, j, k: (i, k)), b.spec(tk, tn, lambda i, j, k: k, lambda i, j, k: j)]
    in_specs += [pl.BlockSpec(bs, im) for _, bs, im in extras]
    scratch = [pltpu.VMEM((tm, tn), F32)] if nk > 1 else []
    return pl.pallas_call(
        functools.partial(_mm_kernel, nk=nk, n_extra=len(extras), epilogue=epilogue),
        grid=(m // tm, n // tn, nk),
        in_specs=in_specs,
        out_specs=pl.BlockSpec((tm, tn), lambda i, j, k: (i, j)),
        out_shape=jax.ShapeDtypeStruct((m, n), out_dtype),
        scratch_shapes=scratch,
        compiler_params=_params(("parallel", "parallel", "arbitrary")),
        name=name,
    )(a, b.arr, *[e[0] for e in extras])


def _row_extra(arr, tn):
    return (arr, (1, tn), lambda i, j, k: (0, j))


def _tile_extra(arr, tm, tn):
    return (arr, (tm, tn), lambda i, j, k: (i, j))


def _head_proj_kernel(*refs, norm, rope, scale):
    a_ref, b_ref, o_ref = refs[0], refs[1], refs[-1]
    extra = list(refs[2:-1])
    g_ref = extra.pop(0) if norm else None
    if rope:
        cos_ref, sa_ref, sb_ref = extra
    a = a_ref[...]
    tn = b_ref.shape[1]
    piece = min(MXU_WIDTH, tn)
    for p0 in range(0, tn, piece):
        acc = jnp.dot(a, b_ref[:, p0:p0 + piece], preferred_element_type=F32)
        for c0 in range(0, piece, HEAD_DIM):
            blk = acc[:, c0:c0 + HEAD_DIM]
            cols = slice(p0 + c0, p0 + c0 + HEAD_DIM)
            if norm:
                ms = jnp.mean(blk * blk, axis=-1, keepdims=True)
                blk = blk * lax.rsqrt(ms + NORM_EPS) * g_ref[:, cols]
            if rope:
                blk = (blk * cos_ref[...] + pltpu.roll(blk, HEAD_DIM - ROPE_HALF, 1) * sa_ref[...]
                       + pltpu.roll(blk, ROPE_HALF, 1) * sb_ref[...])
            if scale != 1.0:
                blk = blk * scale
            o_ref[:, cols] = blk.astype(o_ref.dtype)


def head_proj(h, w, *, gain=None, rope=None, scale=1.0, name):
    m, kd = h.shape
    n = w.cols
    tm, tn = _tile(m, 1024), _tile(n, 1024)
    arrays, specs = [], []
    if gain is not None:
        arrays.append(gain.reshape(1, n))
        specs.append(pl.BlockSpec((1, tn), lambda i, j: (0, j)))
    if rope is not None:
        arrays += list(rope)
        specs += [pl.BlockSpec((tm, HEAD_DIM), lambda i, j: (i, 0))] * len(rope)
    return pl.pallas_call(
        functools.partial(_head_proj_kernel, norm=gain is not None, rope=rope is not None, scale=scale),
        grid=(m // tm, n // tn),
        in_specs=[pl.BlockSpec((tm, kd), lambda i, j: (i, 0)), w.spec(kd, tn, lambda i, j: 0, lambda i, j: j)] + specs,
        out_specs=pl.BlockSpec((tm, tn), lambda i, j: (i, j)),
        out_shape=jax.ShapeDtypeStruct((m, n), BF16),
        compiler_params=_params(("parallel", "parallel")),
        name=name,
    )(h, w.arr, *arrays)


def _rope_table_kernel(pos_ref, freq_ref, cos_ref, sa_ref, sb_ref):
    ang = pos_ref[...] * freq_ref[...]
    lane = lax.broadcasted_iota(jnp.int32, ang.shape, 1)
    c, s = jnp.cos(ang), jnp.sin(ang)
    cos_ref[...] = jnp.where(lane < ROPE_DIM, c, 1.0)
    sa_ref[...] = jnp.where(lane < ROPE_HALF, -s, 0.0)
    sb_ref[...] = jnp.where((lane >= ROPE_HALF) & (lane < ROPE_DIM), s, 0.0)


def rope_tables(positions):
    s = positions.shape[0]
    inv_freq = ROPE_THETA ** (-jnp.arange(ROPE_HALF, dtype=F32) / ROPE_HALF)
    freq_row = jnp.zeros((HEAD_DIM,), F32).at[:ROPE_DIM].set(jnp.concatenate([inv_freq, inv_freq]))
    pos_rep = jnp.broadcast_to(positions.astype(F32)[:, None], (s, HEAD_DIM))
    tm = _tile(s, 1024)
    spec = pl.BlockSpec((tm, HEAD_DIM), lambda i: (i, 0))
    shp = jax.ShapeDtypeStruct((s, HEAD_DIM), F32)
    return pl.pallas_call(
        _rope_table_kernel,
        grid=(s // tm,),
        in_specs=[spec, pl.BlockSpec((1, HEAD_DIM), lambda i: (0, 0))],
        out_specs=[spec, spec, spec],
        out_shape=[shp, shp, shp],
        compiler_params=_params(("parallel",)),
        name="rope_tables",
    )(pos_rep, freq_row.reshape(1, HEAD_DIM))


def _gelu_tanh(x):
    return 0.5 * x * (1.0 + jnp.tanh(np.sqrt(2.0 / np.pi).astype(np.float32) * (x + 0.044715 * (x * x * x))))


def _compress_kernel(a_ref, w_ref, pe_ref, w2_ref, g_ref, o_ref, acc_ref, bias_ref, *, norm):
    l = pl.program_id(0)
    nl = pl.num_programs(0)
    a = a_ref[...]
    w = w_ref[0]
    pe_part = jnp.dot(pe_ref[0], w, preferred_element_type=F32)

    @pl.when(l == 0)
    def _():
        bias_ref[...] = pe_part
        for g in range(NSA_KV_HEADS):
            acc_ref[g] = jnp.dot(a[:, g * HEAD_DIM:(g + 1) * HEAD_DIM], w, preferred_element_type=F32)

    @pl.when(l > 0)
    def _():
        bias_ref[...] += pe_part
        for g in range(NSA_KV_HEADS):
            acc_ref[g] += jnp.dot(a[:, g * HEAD_DIM:(g + 1) * HEAD_DIM], w, preferred_element_type=F32)

    @pl.when(l == nl - 1)
    def _():
        n_chunk = a.shape[0]
        bias = bias_ref[0:1, :HEAD_DIM] + bias_ref[1:2, HEAD_DIM:]
        for g in range(NSA_KV_HEADS):
            p = acc_ref[g]
            hid = p[:, :HEAD_DIM] + pltpu.roll(p[:, HEAD_DIM:], n_chunk - 1, 0) + bias
            out = jnp.dot(_gelu_tanh(hid).astype(BF16), w2_ref[...], preferred_element_type=F32)
            if norm:
                ms = jnp.mean(out * out, axis=-1, keepdims=True)
                out = out * lax.rsqrt(ms + NORM_EPS) * g_ref[...]
            o_ref[:, g * HEAD_DIM:(g + 1) * HEAD_DIM] = out.astype(o_ref.dtype)


def compress(kv, pe, w1, w2, gain):
    s = kv.shape[0]
    n_chunk = s // CMP_STRIDE
    a = kv.reshape(n_chunk, CMP_STRIDE * KV_WIDTH)
    r = CMP_LEN // CMP_STRIDE
    w1r = w1.reshape(r, CMP_STRIDE, HEAD_DIM, HEAD_DIM)
    wcat = jnp.concatenate([w1r[0], w1r[1]], axis=-1).astype(BF16)
    pe_r = pe.reshape(r, CMP_STRIDE, HEAD_DIM).transpose(1, 0, 2)
    pe_l = jnp.zeros((CMP_STRIDE, 8, HEAD_DIM), F32).at[:, :r].set(pe_r).astype(BF16)
    norm = gain is not None
    g = (gain if norm else jnp.ones((HEAD_DIM,), F32)).reshape(1, HEAD_DIM)
    return pl.pallas_call(
        functools.partial(_compress_kernel, norm=norm),
        grid=(CMP_STRIDE,),
        in_specs=[
            pl.BlockSpec((n_chunk, KV_WIDTH), lambda l: (0, l)),
            pl.BlockSpec((1, HEAD_DIM, 2 * HEAD_DIM), lambda l: (l, 0, 0)),
            pl.BlockSpec((1, 8, HEAD_DIM), lambda l: (l, 0, 0)),
            pl.BlockSpec((HEAD_DIM, HEAD_DIM), lambda l: (0, 0)),
            pl.BlockSpec((1, HEAD_DIM), lambda l: (0, 0)),
        ],
        out_specs=pl.BlockSpec((n_chunk, KV_WIDTH), lambda l: (0, 0)),
        out_shape=jax.ShapeDtypeStruct((n_chunk, KV_WIDTH), BF16),
        scratch_shapes=[pltpu.VMEM((NSA_KV_HEADS, n_chunk, 2 * HEAD_DIM), F32), pltpu.VMEM((8, 2 * HEAD_DIM), F32)],
        compiler_params=_params(("arbitrary",)),
        name="nsa_compress",
    )(a, wcat, pe_l, w2.astype(BF16), g)


def _stack_heads(qb):
    return jnp.concatenate([qb[:, j * HEAD_DIM:(j + 1) * HEAD_DIM] for j in range(NSA_GROUP)], axis=0)


def _cmp_body(i, q_ref, k_ref, v_ref, ov_ref, o_ref, sel_ref, width, nbw):
    tq = q_ref.shape[0]
    nb = sel_ref.shape[-1]
    q4 = _stack_heads(q_ref[...])
    s = lax.dot_general(q4, k_ref[:width, :], (((1,), (1,)), ((), ())), preferred_element_type=F32)
    rows = lax.broadcasted_iota(jnp.int32, (NSA_GROUP * tq, width), 0)
    cols = lax.broadcasted_iota(jnp.int32, (NSA_GROUP * tq, width), 1)
    tpos = i * tq + rows % tq
    s = jnp.where(cols * CMP_STRIDE + (CMP_LEN - 1) <= tpos, s, NEG_INF)
    m = jnp.max(s, axis=-1, keepdims=True)
    m = jnp.where(m == NEG_INF, 0.0, m)
    e = jnp.exp2(s - m)
    p = e * (1.0 / jnp.maximum(jnp.sum(e, axis=-1, keepdims=True), 1e-30))
    o = jnp.dot(p.astype(BF16), v_ref[:width, :], preferred_element_type=F32)
    for j in range(NSA_GROUP):
        o_ref[:, j * HEAD_DIM:(j + 1) * HEAD_DIM] = o[j * tq:(j + 1) * tq]

    psum = p[0:tq] + p[tq:2 * tq] + p[2 * tq:3 * tq] + p[3 * tq:4 * tq]
    p_hi = psum.astype(BF16)
    p_lo = (psum - p_hi.astype(F32)).astype(BF16)
    ov = ov_ref[:width, :nbw]
    imp = jnp.dot(p_hi, ov, preferred_element_type=F32) + jnp.dot(p_lo, ov, preferred_element_type=F32)

    blk = lax.broadcasted_iota(jnp.int32, (tq, nbw), 1)
    t = i * tq + lax.broadcasted_iota(jnp.int32, (tq, nbw), 0)
    cur = t // SLC_LEN
    forced = (blk == 0) | (blk == cur) | (blk == cur - 1)
    valid = blk * SLC_LEN <= t
    work = jnp.where(forced, jnp.inf, jnp.where(valid, imp, NEG_INF))
    sel = jnp.zeros((tq, nbw), F32)
    blk_f = blk.astype(F32)
    for _ in range(min(SLC_TOPK, nb)):
        mx = jnp.max(work, axis=-1, keepdims=True)
        first = jnp.min(jnp.where(work == mx, blk_f, float(nb)), axis=-1, keepdims=True)
        pick = (blk_f == first) & (mx > NEG_INF)
        sel = jnp.where(pick, 1.0, sel)
        work = jnp.where(pick, NEG_INF, work)
    bias = jnp.where(sel > 0.5, 0.0, SEL_MASK)
    if nbw < nb:
        bias = jnp.concatenate([bias, jnp.full((tq, nb - nbw), SEL_MASK, F32)], axis=1)
    sel_ref[...] = bias.astype(sel_ref.dtype)


def _cmp_kernel(q_ref, k_ref, v_ref, ov_ref, o_ref, sel_ref, *, col_chunk):
    i = pl.program_id(1)
    tq = q_ref.shape[0]
    n_chunk = k_ref.shape[0]
    nb = sel_ref.shape[-1]
    n_var = n_chunk // col_chunk
    need = (i * tq + tq - CMP_LEN) // CMP_STRIDE + 1
    variant = jnp.minimum((need - 1) // col_chunk, n_var - 1)
    for v in range(n_var):
        width = (v + 1) * col_chunk
        i_max = (CMP_STRIDE * width + CMP_LEN - 1) // tq - 1
        blocks = ((i_max + 1) * tq - 1) // SLC_LEN + 1
        nbw = min(nb, -(-blocks // LANES) * LANES)

        @pl.when(variant == v)
        def _(width=width, nbw=nbw):
            _cmp_body(i, q_ref, k_ref, v_ref, ov_ref, o_ref, sel_ref, width, nbw)


def nsa_cmp_and_select(q, k_cmp, v_cmp):
    s = q.shape[0]
    n_chunk = k_cmp.shape[0]
    nb = s // SLC_LEN
    tq = _tile(s, CMP_Q_TILE)
    c_start = np.arange(n_chunk) * CMP_STRIDE
    s_start = np.arange(nb) * SLC_LEN
    overlap = np.clip(np.minimum(c_start[:, None] + CMP_LEN, s_start[None, :] + SLC_LEN)
                      - np.maximum(c_start[:, None], s_start[None, :]), 0, None).astype(np.float32) / CMP_LEN
    gw = NSA_GROUP * HEAD_DIM
    col_chunk = max(LANES, n_chunk // CMP_WIDTH_VARIANTS)
    return pl.pallas_call(
        functools.partial(_cmp_kernel, col_chunk=col_chunk),
        grid=(NSA_KV_HEADS, s // tq),
        in_specs=[
            pl.BlockSpec((tq, gw), lambda g, i: (i, g)),
            pl.BlockSpec((n_chunk, HEAD_DIM), lambda g, i: (0, g)),
            pl.BlockSpec((n_chunk, HEAD_DIM), lambda g, i: (0, g)),
            pl.BlockSpec((n_chunk, nb), lambda g, i: (0, 0)),
        ],
        out_specs=[
            pl.BlockSpec((tq, gw), lambda g, i: (i, g)),
            pl.BlockSpec((None, tq, nb), lambda g, i: (g, i, 0)),
        ],
        out_shape=[jax.ShapeDtypeStruct((s, NSA_WIDTH), F32), jax.ShapeDtypeStruct((NSA_KV_HEADS, s, nb), BF16)],
        compiler_params=_params(("parallel", "parallel")),
        name="nsa_cmp_select",
    )(q, k_cmp, v_cmp, jnp.asarray(overlap, BF16))


def _flash_tile(q2, k2, v, m_ref, acc_ref, mask=None):
    tk = k2.shape[0]
    s = lax.dot_general(q2, k2, (((1,), (1,)), ((), ())), preferred_element_type=F32)
    if mask is not None:
        s = jnp.where(mask, s, NEG_INF)
    m_old = m_ref[...]
    m_new = jnp.maximum(m_old, jnp.max(s, axis=-1, keepdims=True))
    alpha = jnp.exp2(m_old - m_new)
    p = jnp.exp2(s - jnp.concatenate([m_new] * (tk // LANES), axis=1))
    v2 = jnp.concatenate([v, jnp.ones((tk, LANES), v.dtype)], axis=1)
    acc_ref[...] = jnp.concatenate([alpha, alpha], axis=1) * acc_ref[...] + jnp.dot(
        p.astype(BF16), v2, preferred_element_type=F32)
    m_ref[...] = m_new


def _init_flash(m_ref, acc_ref):
    m_ref[...] = jnp.full(m_ref.shape, NEG_INF, F32)
    acc_ref[...] = jnp.zeros(acc_ref.shape, F32)


def _flash_result(acc_ref):
    acc = acc_ref[...]
    return acc[:, :HEAD_DIM] * (1.0 / acc[:, HEAD_DIM:])


def _for_each_tile(n, tile_fn):
    def body(jj, carry):
        for u in range(KV_UNROLL):
            tile_fn(jj * KV_UNROLL + u)
        return carry

    lax.fori_loop(0, n // KV_UNROLL, body, 0)
    base = (n // KV_UNROLL) * KV_UNROLL
    g = KV_UNROLL // 2
    while g >= 1:
        take = (n & g) != 0

        @pl.when(take)
        def _(base=base, g=g):
            for u in range(g):
                tile_fn(base + u)

        base = base + jnp.where(take, g, 0)
        g //= 2


def _slc_kernel(q_ref, bias_ref, k_ref, kx_ref, v_ref, o_ref, q2_ref, m_ref, acc_ref, *, tk):
    i = pl.program_id(1)
    tq = q_ref.shape[0]
    cw = kx_ref.shape[1]
    _init_flash(m_ref, acc_ref)
    q4 = _stack_heads(q_ref[...])
    for c in range(q2_ref.shape[0]):
        bias = bias_ref[:, c * cw:(c + 1) * cw]
        q2_ref[c] = jnp.concatenate([q4, jnp.concatenate([bias] * NSA_GROUP, axis=0)], axis=1)

    def tile(j, masked):
        off = pl.multiple_of(j * tk, tk)
        k2 = jnp.concatenate([k_ref[pl.ds(off, tk), :], kx_ref[pl.ds(off, tk), :]], axis=1)
        q2 = q2_ref[(j * tk) // (SLC_LEN * cw)]
        mask = None
        if masked:
            rows = lax.broadcasted_iota(jnp.int32, (NSA_GROUP * tq, tk), 0)
            cols = lax.broadcasted_iota(jnp.int32, (NSA_GROUP * tq, tk), 1)
            mask = off + cols <= i * tq + rows % tq
        _flash_tile(q2, k2, v_ref[pl.ds(off, tk), :], m_ref, acc_ref, mask)

    last = (i * tq) // tk
    _for_each_tile(last, lambda j: tile(j, False))
    tile(last, True)
    o = _flash_result(acc_ref)
    for h in range(NSA_GROUP):
        o_ref[:, h * HEAD_DIM:(h + 1) * HEAD_DIM] = o[h * tq:(h + 1) * tq]


def nsa_selected(q, k_slc, v_slc, sel_bias):
    s = q.shape[0]
    nb = s // SLC_LEN
    cw = min(nb, LANES)
    tq, tk = Q_BLOCK, _tile(s, 512)
    key_blk = jnp.arange(s, dtype=jnp.int32)[:, None] // SLC_LEN
    key_onehot = (key_blk % cw == jnp.arange(cw, dtype=jnp.int32)[None, :]).astype(BF16)
    gw = NSA_GROUP * HEAD_DIM
    rows = NSA_GROUP * tq
    return pl.pallas_call(
        functools.partial(_slc_kernel, tk=tk),
        grid=(NSA_KV_HEADS, s // tq),
        in_specs=[
            pl.BlockSpec((tq, gw), lambda g, i: (i, g)),
            pl.BlockSpec((None, tq, nb), lambda g, i: (g, i, 0)),
            pl.BlockSpec((s, HEAD_DIM), lambda g, i: (0, g)),
            pl.BlockSpec((s, cw), lambda g, i: (0, 0)),
            pl.BlockSpec((s, HEAD_DIM), lambda g, i: (0, g)),
        ],
        out_specs=pl.BlockSpec((tq, gw), lambda g, i: (i, g)),
        out_shape=jax.ShapeDtypeStruct((s, NSA_WIDTH), F32),
        scratch_shapes=[pltpu.VMEM((nb // cw, rows, HEAD_DIM + cw), BF16), pltpu.VMEM((rows, LANES), F32),
                        pltpu.VMEM((rows, 2 * HEAD_DIM), F32)],
        compiler_params=_params(("parallel", "arbitrary")),
        name="nsa_selected",
    )(q, sel_bias, k_slc, key_onehot, v_slc)


WIN_TILES = WIN_LEN // Q_BLOCK + 1


def _win_kernel(*refs):
    q_ref = refs[0]
    k_refs = refs[1:1 + WIN_TILES]
    v_refs = refs[1 + WIN_TILES:1 + 2 * WIN_TILES]
    ocmp_ref, oslc_ref, gate_ref, o_ref = refs[1 + 2 * WIN_TILES:]
    i = pl.program_id(0)
    tq = q_ref.shape[0]
    q4 = _stack_heads(q_ref[...])
    kcat = jnp.concatenate([r[...] for r in k_refs], axis=0)
    vcat = jnp.concatenate([r[...] for r in v_refs], axis=0)
    s = lax.dot_general(q4, kcat, (((1,), (1,)), ((), ())), preferred_element_type=F32)
    rows = lax.broadcasted_iota(jnp.int32, s.shape, 0)
    cols = lax.broadcasted_iota(jnp.int32, s.shape, 1)
    tpos = i * tq + rows % tq
    wpos = (i - (WIN_TILES - 1)) * tq + cols
    ok = (wpos <= tpos) & (wpos > tpos - WIN_LEN) & (wpos >= 0)
    s = jnp.where(ok, s, NEG_INF)
    m = jnp.max(s, axis=-1, keepdims=True)
    e = jnp.exp2(s - m)
    p = e * (1.0 / jnp.sum(e, axis=-1, keepdims=True))
    o_win = jnp.dot(p.astype(BF16), vcat, preferred_element_type=F32)
    gates = gate_ref[...]
    for h in range(NSA_GROUP):
        sl = slice(h * HEAD_DIM, (h + 1) * HEAD_DIM)
        o = (gates[:, 3 * h:3 * h + 1] * ocmp_ref[:, sl] + gates[:, 3 * h + 1:3 * h + 2] * oslc_ref[:, sl]
             + gates[:, 3 * h + 2:3 * h + 3] * o_win[h * tq:(h + 1) * tq])
        o_ref[:, sl] = o.astype(o_ref.dtype)


def nsa_window_combine(q, k_win, v_win, o_cmp, o_slc, gates):
    s = q.shape[0]
    tq = Q_BLOCK
    gw = NSA_GROUP * HEAD_DIM
    back = WIN_TILES - 1
    kv_specs = [pl.BlockSpec((tq, HEAD_DIM), lambda i, g, d=d: (jnp.maximum(i - back + d, 0), g))
                for d in range(WIN_TILES)]
    blk = pl.BlockSpec((tq, gw), lambda i, g: (i, g))
    return pl.pallas_call(
        _win_kernel,
        grid=(s // tq, NSA_KV_HEADS),
        in_specs=[blk] + kv_specs + kv_specs + [blk, blk, pl.BlockSpec((None, tq, 3 * NSA_GROUP), lambda i, g: (g, i, 0))],
        out_specs=blk,
        out_shape=jax.ShapeDtypeStruct((s, NSA_WIDTH), BF16),
        compiler_params=_params(("parallel", "parallel")),
        name="nsa_window_combine",
    )(q, *([k_win] * WIN_TILES), *([v_win] * WIN_TILES), o_cmp, o_slc, gates)


def _conv_kernel(b_ref, c_ref, h_ref, ch_ref, hh_ref, w_ref, o_ref):
    i = pl.program_id(0)
    u = c_ref[...] * h_ref[...]
    halo = jnp.where(i > 0, ch_ref[...] * hh_ref[...], 0.0)
    x = jnp.concatenate([halo, u], axis=0)
    w = w_ref[...]
    y = (w[2:3] * x + w[1:2] * pltpu.roll(x, 1, 0) + w[0:1] * pltpu.roll(x, 2, 0))[CONV_HALO:]
    o_ref[...] = (b_ref[...] * y).astype(o_ref.dtype)


def short_conv(z, conv_w):
    s = z.shape[0]
    cw = conv_w.shape[1]
    tm, tc = _tile(s, 512), _tile(cw, 512)
    nc = cw // tc
    hb = tm // CONV_HALO
    halo = lambda off: pl.BlockSpec((CONV_HALO, tc), lambda i, j: (jnp.maximum(i * hb - 1, 0), off * nc + j))
    main = lambda off: pl.BlockSpec((tm, tc), lambda i, j: (i, off * nc + j))
    w8 = jnp.zeros((8, cw), F32).at[:CONV_K].set(conv_w)
    return pl.pallas_call(
        _conv_kernel,
        grid=(s // tm, nc),
        in_specs=[main(0), main(1), main(2), halo(1), halo(2), pl.BlockSpec((8, tc), lambda i, j: (0, j))],
        out_specs=pl.BlockSpec((tm, tc), lambda i, j: (i, j)),
        out_shape=jax.ShapeDtypeStruct((s, cw), BF16),
        compiler_params=_params(("parallel", "parallel")),
        name="short_conv",
    )(z, z, z, z, z, w8)


def _split3(x):
    hi = x.astype(BF16)
    r1 = x - hi.astype(F32)
    mid = r1.astype(BF16)
    lo = (r1 - mid.astype(F32)).astype(BF16)
    return hi, mid, lo


def _cumsum_kernel(x_ref, hi_ref, mid_ref, lo_ref, carry_ref):
    @pl.when(pl.program_id(0) == 0)
    def _():
        carry_ref[...] = jnp.zeros(carry_ref.shape, F32)

    x = x_ref[...]
    t = x.shape[0]
    tri = (lax.broadcasted_iota(jnp.int32, (t, t), 0) >= lax.broadcasted_iota(jnp.int32, (t, t), 1)).astype(BF16)
    c = sum(jnp.dot(tri, part, preferred_element_type=F32) for part in _split3(x)) + carry_ref[0:1]
    carry_ref[...] = jnp.broadcast_to(c[t - 1:t], carry_ref.shape)
    hi_ref[...], mid_ref[...], lo_ref[...] = _split3(c * LOG2E)


def cumsum_log2_split(x):
    s, w = x.shape
    t = _tile(s, 512)
    spec = pl.BlockSpec((t, w), lambda i: (i, 0))
    shp = jax.ShapeDtypeStruct((s, w), BF16)
    return pl.pallas_call(
        _cumsum_kernel,
        grid=(s // t,),
        in_specs=[spec],
        out_specs=[spec, spec, spec],
        out_shape=[shp, shp, shp],
        scratch_shapes=[pltpu.VMEM((8, w), F32)],
        compiler_params=_params(("arbitrary",)),
        name="cumsum",
    )(x)


def _fox_kernel(q_ref, qx_ref, k_ref, kx_ref, v_ref, cq_ref, ck_ref, o_ref, m_ref, acc_ref, kn_ref, *, t):
    i = pl.program_id(1)
    _init_flash(m_ref, acc_ref)
    q = q_ref[...]
    q2 = jnp.concatenate([q, qx_ref[...]], axis=1)

    @pl.when(i == 0)
    def _():
        def body(j, mx):
            kk = k_ref[pl.ds(pl.multiple_of(j * t, t), t), :].astype(F32)
            return jnp.maximum(mx, jnp.max(jnp.sum(kk * kk, axis=1, keepdims=True), axis=0, keepdims=True))
        kn_ref[...] = jnp.broadcast_to(lax.fori_loop(0, k_ref.shape[0] // t, body, jnp.zeros((1, 1), F32)), kn_ref.shape)

    qq = q.astype(F32)
    bound = jnp.max(jnp.sqrt(jnp.sum(qq * qq, axis=1, keepdims=True) * kn_ref[0:1, 0:1]), axis=0, keepdims=True)
    gap = cq_ref[pl.ds(i, 1), :] - ck_ref[...]
    lane = lax.broadcasted_iota(jnp.int32, gap.shape, 1)
    keep = (gap >= -(2.0 * bound + FOX_PRUNE_BITS)) & (lane < i)
    n_keep = jnp.sum(jnp.where(keep, 1, 0))
    first = i - n_keep

    def tile(j, masked):
        off = pl.multiple_of(j * t, t)
        k2 = jnp.concatenate([k_ref[pl.ds(off, t), :], kx_ref[pl.ds(off, t), :]], axis=1)
        mask = None
        if masked:
            mask = lax.broadcasted_iota(jnp.int32, (t, t), 1) <= lax.broadcasted_iota(jnp.int32, (t, t), 0)
        _flash_tile(q2, k2, v_ref[pl.ds(off, t), :], m_ref, acc_ref, mask)

    _for_each_tile(n_keep, lambda jj: tile(first + jj, False))
    tile(i, True)
    o_ref[...] = _flash_result(acc_ref).astype(o_ref.dtype)


def fox_attention(q, k, v, c_parts):
    s = q.shape[0]
    t = _tile(s, 512)
    parts = jnp.stack([part[:, :FOX_HEADS] for part in c_parts], axis=-1)
    ones = jnp.ones_like(parts)
    widen = lambda a: jnp.pad(a, ((0, 0), (0, 0), (0, HEAD_DIM - a.shape[-1]))).reshape(s, FOX_WIDTH)
    qx = widen(jnp.concatenate([parts, ones], axis=-1))
    kx = widen(jnp.concatenate([ones, -parts], axis=-1))
    nt = s // t
    assert nt <= LANES
    c = sum(part[:, :FOX_HEADS].astype(F32) for part in c_parts)
    cq_first = jnp.broadcast_to(c[0::t].T[:, :, None], (FOX_HEADS, nt, LANES))
    ck_last = jnp.pad(c[t - 1::t].T, ((0, 0), (0, LANES - nt))).reshape(FOX_HEADS, 1, LANES)
    tile_spec = pl.BlockSpec((t, HEAD_DIM), lambda h, i: (i, h))
    full_spec = pl.BlockSpec((s, HEAD_DIM), lambda h, i: (0, h))
    return pl.pallas_call(
        functools.partial(_fox_kernel, t=t),
        grid=(FOX_HEADS, nt),
        in_specs=[tile_spec, tile_spec, full_spec, full_spec, full_spec,
                  pl.BlockSpec((None, nt, LANES), lambda h, i: (h, 0, 0)),
                  pl.BlockSpec((None, 1, LANES), lambda h, i: (h, 0, 0))],
        out_specs=tile_spec,
        out_shape=jax.ShapeDtypeStruct((s, FOX_WIDTH), BF16),
        scratch_shapes=[pltpu.VMEM((t, LANES), F32), pltpu.VMEM((t, 2 * HEAD_DIM), F32), pltpu.VMEM((8, LANES), F32)],
        compiler_params=_params(("parallel", "arbitrary")),
        name="fox_attention",
    )(q, qx, k, kx, v, cq_first, ck_last)


def _pool_kernel(u_ref, halo_ref, w_ref, scale_ref, o_ref):
    i = pl.program_id(0)
    tm = u_ref.shape[0]
    u = u_ref[...]
    halo = jnp.where(i > 0, halo_ref[...], 0.0)
    x = jnp.concatenate([halo, u], axis=0)
    t1 = i * tm + lax.broadcasted_iota(jnp.int32, (tm, POOL_GROUP), 0) + 1
    for g, win in enumerate(POOL_WINDOWS):
        sl = slice(g * POOL_GROUP, (g + 1) * POOL_GROUP)
        acc = x[:, sl]
        span = 1
        while span < win:
            acc = acc + pltpu.roll(acc, span, 0)
            span *= 2
        cnt = jnp.minimum(t1, win).astype(F32)
        d = acc[POOL_HALO:] / cnt - u[:, sl]
        y = jnp.dot(d.astype(BF16), w_ref[g], preferred_element_type=F32)
        o_ref[:, sl] = (y * scale_ref[:, sl]).astype(o_ref.dtype)


def multiscale_pool(u, w_pool, scale):
    s, pw = u.shape
    tm = _tile(s, 512)
    hb = tm // POOL_HALO
    return pl.pallas_call(
        _pool_kernel,
        grid=(s // tm,),
        in_specs=[
            pl.BlockSpec((tm, pw), lambda i: (i, 0)),
            pl.BlockSpec((POOL_HALO, pw), lambda i: (jnp.maximum(i * hb - 1, 0), 0)),
            pl.BlockSpec(w_pool.shape, lambda i: (0, 0, 0)),
            pl.BlockSpec((1, pw), lambda i: (0, 0)),
        ],
        out_specs=pl.BlockSpec((tm, pw), lambda i: (i, 0)),
        out_shape=jax.ShapeDtypeStruct((s, pw), BF16),
        compiler_params=_params(("parallel",)),
        name="multiscale_pool",
    )(u, u, w_pool.astype(BF16), scale.reshape(1, pw))


def _swiglu_kernel(h_ref, w1_ref, w3_ref, o_ref):
    h = h_ref[...]
    a = jnp.dot(h, w1_ref[...].astype(BF16), preferred_element_type=F32)
    b = jnp.dot(h, w3_ref[...].astype(BF16), preferred_element_type=F32)
    o_ref[...] = (a * jax.nn.sigmoid(a) * b).astype(o_ref.dtype)


def swiglu_up(h, w1, w3):
    m, d = h.shape
    n = w1.cols
    tm, tn = _tile(m, 2048), _tile(n, 512)
    specs = [w.spec(d, tn, lambda i, j: 0, lambda i, j: j) for w in (w1, w3)]
    return pl.pallas_call(
        _swiglu_kernel,
        grid=(m // tm, n // tn),
        in_specs=[_lhs_spec((tm, d), lambda i, j: (i, 0), True)] + specs,
        out_specs=pl.BlockSpec((tm, tn), lambda i, j: (i, j)),
        out_shape=jax.ShapeDtypeStruct((m, n), BF16),
        compiler_params=_params(("parallel", "parallel")),
        name="swiglu_up",
    )(h, w1.arr, w3.arr)


def _resid_epilogue(acc, x_ref):
    return x_ref[...] + acc


def _gate_epilogue(acc, x_ref, p_ref, wp_ref):
    return x_ref[...] + jax.nn.sigmoid(acc) * jnp.dot(p_ref[...], wp_ref[...].astype(BF16),
                                                      preferred_element_type=F32)


def _logsig_epilogue(acc, b_ref):
    y = -(acc + b_ref[...])
    return -(jnp.maximum(y, 0.0) + jnp.log1p(jnp.exp(-jnp.abs(y))))


def _sigmoid_epilogue(acc):
    return jax.nn.sigmoid(acc)


def matmul_resid(a, b, x, *, tk=4096, name):
    tm, tn = _tile(a.shape[0], 1024), _tile(b.cols, 512)
    return matmul(a, b, out_dtype=F32, epilogue=_resid_epilogue, extras=[_tile_extra(x, tm, tn)], tm=tm, tn=tn, tk=tk,
                  name=name)


def _out_proj_kernel(a1_ref, a2_ref, w1_ref, w2_ref, x_ref, o_ref):
    acc = jnp.dot(a1_ref[...], w1_ref[...], preferred_element_type=F32)
    acc += jnp.dot(a2_ref[...], w2_ref[...], preferred_element_type=F32)
    o_ref[...] = x_ref[...] + acc


def out_proj_resid(a1, a2, w, x, *, name):
    m, k1 = a1.shape
    k2 = a2.shape[1]
    assert k1 % k2 == 0 and w.k == k1 + k2
    n = w.cols
    tm, tn = _tile(m, 1024), _tile(n, 512)
    ij = lambda i, j: (i, j)
    return pl.pallas_call(
        _out_proj_kernel,
        grid=(m // tm, n // tn),
        in_specs=[pl.BlockSpec((tm, k1), lambda i, j: (i, 0)), pl.BlockSpec((tm, k2), lambda i, j: (i, 0)),
                  w.spec(k1, tn, lambda i, j: 0, lambda i, j: j), w.spec(k2, tn, lambda i, j: k1 // k2, lambda i, j: j),
                  pl.BlockSpec((tm, tn), ij)],
        out_specs=pl.BlockSpec((tm, tn), ij),
        out_shape=jax.ShapeDtypeStruct((m, n), F32),
        compiler_params=_params(("parallel", "parallel")),
        name=name,
    )(a1, a2, w.arr, w.arr, x)


def _k_tile(k, cap):
    for t in range(cap - cap % LANES, 0, -LANES):
        if k % t == 0:
            return t
    return k


FFN_DOWN_TK_CAP = 5632


def ffn_and_embed(x, p_i, layer, f_norm, w1, w3, w2, e_norm, w_gate, w_proj):
    h = rmsnorm_bf16(x, f_norm)
    u = swiglu_up(h, Weight(w1, layer), Weight(w3, layer))
    x = matmul_resid(u, Weight(w2, layer), x, tk=_k_tile(w2.shape[-2], FFN_DOWN_TK_CAP), name="ffn_down")
    h = rmsnorm_bf16(x, e_norm)
    m, d = x.shape
    tm, tn = _tile(m, 1024), _tile(d, 512)
    pd = p_i.shape[1]
    extras = [_tile_extra(x, tm, tn), (p_i.astype(BF16), (tm, pd), lambda i, j, k: (i, 0)),
              (w_proj, (None, pd, tn), lambda i, j, k: (layer, 0, j))]
    return matmul(h, Weight(w_gate, layer), out_dtype=F32, epilogue=_gate_epilogue, extras=extras, tm=tm, tn=tn,
                  name="embed_gate")


def even_mixer(x, rope, layer, norm_g, w_in, q_norm, k_norm, cmp_pe, cmp_w1, cmp_w2, conv_w, w_out):
    s = x.shape[0]
    h = rmsnorm_bf16(x, norm_g)
    o_kv = NSA_WIDTH
    o_gate = o_kv + 6 * KV_WIDTH
    o_conv = o_gate + 3 * NSA_HEADS
    kv_cols = lambda i: slice(o_kv + i * KV_WIDTH, o_kv + (i + 1) * KV_WIDTH)
    tile_gain = lambda g, n: jnp.tile(g, n // HEAD_DIM)

    q = head_proj(h, Weight(w_in, layer, 0, NSA_WIDTH), gain=tile_gain(q_norm, NSA_WIDTH), rope=rope, scale=Q_SCALE,
                  name="nsa_q_proj")
    k_cmp_in = head_proj(h, Weight(w_in, layer, o_kv, KV_WIDTH), rope=rope, name="nsa_kcmp_proj")
    w_v = jnp.concatenate([w_in[layer, :, kv_cols(i)] for i in (1, 3, 5)], axis=1)
    v_all = matmul(h, w_v, out_dtype=BF16, name="nsa_v_proj")
    v_cmp_in, v_slc, v_win = (v_all[:, i * KV_WIDTH:(i + 1) * KV_WIDTH] for i in range(3))
    w_k = jnp.concatenate([w_in[layer, :, kv_cols(i)] for i in (2, 4)], axis=1)
    g_k = jnp.concatenate([tile_gain(k_norm[1], KV_WIDTH), tile_gain(k_norm[2], KV_WIDTH)])
    k_both = head_proj(h, Weight(w_k), gain=g_k, rope=rope, name="nsa_k_proj")
    k_slc, k_win = k_both[:, :KV_WIDTH], k_both[:, KV_WIDTH:]
    gates = matmul(h, Weight(w_in, layer, o_gate, LANES), out_dtype=F32, epilogue=_sigmoid_epilogue,
                   name="nsa_gate_proj")[:, :3 * NSA_HEADS]
    gates = gates.reshape(s, NSA_KV_HEADS, 3 * NSA_GROUP).transpose(1, 0, 2)
    z_conv = matmul(h, w_in[layer, :, o_conv:], out_dtype=F32, name="conv_proj")

    k_cmp = compress(k_cmp_in, cmp_pe[0], cmp_w1[0], cmp_w2[0], k_norm[0])
    v_cmp = compress(v_cmp_in, cmp_pe[1], cmp_w1[1], cmp_w2[1], None)
    o_cmp, sel = nsa_cmp_and_select(q, k_cmp, v_cmp)
    o_slc = nsa_selected(q, k_slc, v_slc, sel)
    o_nsa = nsa_window_combine(q, k_win, v_win, o_cmp, o_slc, gates)
    o_conv_out = short_conv(z_conv, conv_w)
    return out_proj_resid(o_nsa, o_conv_out, Weight(w_out, layer), x, name="even_out_proj")


def odd_mixer(x, layer, norm_g, w_in, f_bias, q_norm, k_norm, pool_w, pool_scale, w_out):
    h = rmsnorm_bf16(x, norm_g)
    o_fgate = 3 * FOX_WIDTH
    o_pool = o_fgate + FOX_HEADS
    tile_gain = lambda g: jnp.tile(g, FOX_WIDTH // HEAD_DIM)
    q = head_proj(h, Weight(w_in, layer, 0, FOX_WIDTH), gain=tile_gain(q_norm), scale=Q_SCALE, name="fox_q_proj")
    k = head_proj(h, Weight(w_in, layer, FOX_WIDTH, FOX_WIDTH), gain=tile_gain(k_norm), name="fox_k_proj")
    v = matmul(h, Weight(w_in, layer, 2 * FOX_WIDTH, FOX_WIDTH), out_dtype=BF16, name="fox_v_proj")
    b_f = jnp.pad(f_bias, (0, LANES - FOX_HEADS)).reshape(1, LANES)
    log_f = matmul(h, Weight(w_in, layer, o_fgate, LANES), out_dtype=F32, epilogue=_logsig_epilogue,
                   extras=[_row_extra(b_f, LANES)], name="fox_gate_proj")
    u = matmul(h, w_in[layer, :, o_pool:], out_dtype=F32, name="pool_proj")
    o_fox = fox_attention(q, k, v, cumsum_log2_split(log_f))
    o_pool_out = multiscale_pool(u, pool_w, pool_scale)
    return out_proj_resid(o_fox, o_pool_out, Weight(w_out, layer), x, name="odd_out_proj")


def kernel(x, p, positions, a_norm, a_w_in, a_q_norm, a_k_norm, a_cmp_pe, a_cmp_w1, a_cmp_w2, a_conv_w, a_w_out, b_norm, b_w_in, b_f_bias, b_q_norm, b_k_norm, b_pool_w, b_pool_scale, b_w_out, f_norm, f_w1, f_w3, f_w2, e_norm, e_w_gate, e_w_proj):
    batch, s, d = x.shape
    depth = p.shape[0]
    a_w_in, a_w_out, b_w_in, b_w_out, f_w2, e_w_gate, e_w_proj = (
        w.astype(BF16) for w in (a_w_in, a_w_out, b_w_in, b_w_out, f_w2, e_w_gate, e_w_proj))
    outs = []
    for b in range(batch):
        xb = x[b]
        rope = rope_tables(positions[b])
        for i in range(depth):
            j = i // 2
            if i % 2 == 0:
                xb = even_mixer(xb, rope, j, a_norm[j], a_w_in, a_q_norm[j], a_k_norm[j], a_cmp_pe[j], a_cmp_w1[j],
                                a_cmp_w2[j], a_conv_w[j], a_w_out)
            else:
                xb = odd_mixer(xb, j, b_norm[j], b_w_in, b_f_bias[j], b_q_norm[j], b_k_norm[j], b_pool_w[j],
                               b_pool_scale[j], b_w_out)
            xb = ffn_and_embed(xb, p[i, b], i, f_norm[i], f_w1, f_w3, f_w2, e_norm[i], e_w_gate, e_w_proj)
        outs.append(xb)
    return jnp.stack(outs, axis=0)
```

```python
import functools
from typing import NamedTuple, Optional

import jax
import jax.numpy as jnp
import numpy as np
from jax import lax
from jax.experimental import pallas as pl
from jax.experimental.pallas import tpu as pltpu

HEAD_DIM = 128
ROPE_DIM = HEAD_DIM // 4
ROPE_HALF = ROPE_DIM // 2
ROPE_THETA = 500000.0
NORM_EPS = 1e-6
ATTN_SCALE = HEAD_DIM ** -0.5
Q_BLOCK = 128

NSA_HEADS = 16
NSA_KV_HEADS = 4
NSA_GROUP = NSA_HEADS // NSA_KV_HEADS
NSA_WIDTH = NSA_HEADS * HEAD_DIM
KV_WIDTH = NSA_KV_HEADS * HEAD_DIM
CMP_LEN = 32
CMP_STRIDE = 16
SLC_LEN = 64
SLC_TOPK = 16
WIN_LEN = 512
CONV_K = 3

FOX_HEADS = 24
FOX_WIDTH = FOX_HEADS * HEAD_DIM
POOL_WINDOWS = (2, 4, 8, 16)
POOL_GROUP = 256
POOL_HALO = 16
CONV_HALO = 8

LANES = 128
MXU_WIDTH = 256
VMEM_LIMIT_BYTES = 56 * 1024 * 1024
BF16 = jnp.bfloat16
F32 = jnp.float32
NEG_INF = float("-inf")
LOG2E = 1.4426950408889634
Q_SCALE = ATTN_SCALE * LOG2E
SEL_MASK = -1e30
KV_UNROLL = 8
FOX_PRUNE_BITS = 100.0
CMP_WIDTH_VARIANTS = 4
CMP_Q_TILE = 256
SLC_Q_TILE = 256


def _params(sem):
    return pltpu.CompilerParams(dimension_semantics=sem, vmem_limit_bytes=VMEM_LIMIT_BYTES)


def _tile(dim, pref):
    if dim <= pref:
        return dim
    t = pref
    while dim % t:
        t //= 2
    return t


def _rmsnorm_kernel(x_ref, g_ref, o_ref):
    x = x_ref[...]
    ms = jnp.mean(x * x, axis=-1, keepdims=True)
    o_ref[...] = (x * lax.rsqrt(ms + NORM_EPS) * g_ref[...]).astype(o_ref.dtype)


def rmsnorm_bf16(x, g):
    s, d = x.shape
    tm = _tile(s, 256)
    return pl.pallas_call(
        _rmsnorm_kernel,
        grid=(s // tm,),
        in_specs=[pl.BlockSpec((tm, d), lambda i: (i, 0)), pl.BlockSpec((1, d), lambda i: (0, 0))],
        out_specs=pl.BlockSpec((tm, d), lambda i: (i, 0)),
        out_shape=jax.ShapeDtypeStruct((s, d), BF16),
        compiler_params=_params(("parallel",)),
        name="rmsnorm",
    )(x, g.reshape(1, d))


def _mm_kernel(*refs, nk, n_extra, epilogue):
    a_ref, b_ref = refs[0], refs[1]
    extra = refs[2:2 + n_extra]
    o_ref = refs[2 + n_extra]
    part = jnp.dot(a_ref[...], b_ref[...].astype(BF16), preferred_element_type=F32)
    if nk == 1:
        o_ref[...] = epilogue(part, *extra).astype(o_ref.dtype)
        return
    acc_ref = refs[3 + n_extra]
    k = pl.program_id(2)

    @pl.when(k == 0)
    def _():
        acc_ref[...] = part

    @pl.when(k > 0)
    def _():
        acc_ref[...] += part

    @pl.when(k == nk - 1)
    def _():
        o_ref[...] = epilogue(acc_ref[...], *extra).astype(o_ref.dtype)


class Weight(NamedTuple):
    arr: jax.Array
    layer: Optional[int] = None
    col0: int = 0
    n: Optional[int] = None

    @property
    def k(self):
        return self.arr.shape[-2]

    @property
    def cols(self):
        return self.n if self.n is not None else self.arr.shape[-1] - self.col0

    def spec(self, tk, tn, k_of, j_of):
        assert self.col0 % tn == 0
        j0 = self.col0 // tn
        if self.layer is None:
            return pl.BlockSpec((tk, tn), lambda *g: (k_of(*g), j0 + j_of(*g)))
        return pl.BlockSpec((None, tk, tn), lambda *g: (self.layer, k_of(*g), j0 + j_of(*g)))


def _lhs_spec(block, index_map, single_buffer):
    if single_buffer:
        return pl.BlockSpec(block, index_map, pipeline_mode=pl.Buffered(1))
    return pl.BlockSpec(block, index_map)


def matmul(a, b, *, out_dtype, epilogue=None, extras=(), tm=1024, tn=512, tk=4096, name="matmul"):
    if not isinstance(b, Weight):
        b = Weight(b)
    m, kd = a.shape
    n = b.cols
    tm, tn, tk = _tile(m, tm), _tile(n, tn), _tile(kd, tk)
    nk = kd // tk
    if epilogue is None:
        epilogue = lambda acc: acc
    in_specs = [pl.BlockSpec((tm, tk), lambda i, j, k: (i, k)), b.spec(tk, tn, lambda i, j, k: k, lambda i, j, k: j)]
    in_specs += [pl.BlockSpec(bs, im) for _, bs, im in extras]
    scratch = [pltpu.VMEM((tm, tn), F32)] if nk > 1 else []
    return pl.pallas_call(
        functools.partial(_mm_kernel, nk=nk, n_extra=len(extras), epilogue=epilogue),
        grid=(m // tm, n // tn, nk),
        in_specs=in_specs,
        out_specs=pl.BlockSpec((tm, tn), lambda i, j, k: (i, j)),
        out_shape=jax.ShapeDtypeStruct((m, n), out_dtype),
        scratch_shapes=scratch,
        compiler_params=_params(("parallel", "parallel", "arbitrary")),
        name=name,
    )(a, b.arr, *[e[0] for e in extras])


def _row_extra(arr, tn):
    return (arr, (1, tn), lambda i, j, k: (0, j))


def _tile_extra(arr, tm, tn):
    return (arr, (tm, tn), lambda i, j, k: (i, j))


def _head_proj_kernel(*refs, norm, rope, scale):
    a_ref, b_ref, o_ref = refs[0], refs[1], refs[-1]
    extra = list(refs[2:-1])
    g_ref = extra.pop(0) if norm else None
    if rope:
        cos_ref, sa_ref, sb_ref = extra
    a = a_ref[...]
    tn = b_ref.shape[1]
    piece = min(MXU_WIDTH, tn)
    for p0 in range(0, tn, piece):
        acc = jnp.dot(a, b_ref[:, p0:p0 + piece], preferred_element_type=F32)
        for c0 in range(0, piece, HEAD_DIM):
            blk = acc[:, c0:c0 + HEAD_DIM]
            cols = slice(p0 + c0, p0 + c0 + HEAD_DIM)
            if norm:
                ms = jnp.mean(blk * blk, axis=-1, keepdims=True)
                blk = blk * lax.rsqrt(ms + NORM_EPS) * g_ref[:, cols]
            if rope:
                blk = (blk * cos_ref[...] + pltpu.roll(blk, HEAD_DIM - ROPE_HALF, 1) * sa_ref[...]
                       + pltpu.roll(blk, ROPE_HALF, 1) * sb_ref[...])
            if scale != 1.0:
                blk = blk * scale
            o_ref[:, cols] = blk.astype(o_ref.dtype)


def head_proj(h, w, *, gain=None, rope=None, scale=1.0, name):
    m, kd = h.shape
    n = w.cols
    tm, tn = _tile(m, 1024), _tile(n, 1024)
    arrays, specs = [], []
    if gain is not None:
        arrays.append(gain.reshape(1, n))
        specs.append(pl.BlockSpec((1, tn), lambda i, j: (0, j)))
    if rope is not None:
        arrays += list(rope)
        specs += [pl.BlockSpec((tm, HEAD_DIM), lambda i, j: (i, 0))] * len(rope)
    return pl.pallas_call(
        functools.partial(_head_proj_kernel, norm=gain is not None, rope=rope is not None, scale=scale),
        grid=(m // tm, n // tn),
        in_specs=[pl.BlockSpec((tm, kd), lambda i, j: (i, 0)), w.spec(kd, tn, lambda i, j: 0, lambda i, j: j)] + specs,
        out_specs=pl.BlockSpec((tm, tn), lambda i, j: (i, j)),
        out_shape=jax.ShapeDtypeStruct((m, n), BF16),
        compiler_params=_params(("parallel", "parallel")),
        name=name,
    )(h, w.arr, *arrays)


def _rope_table_kernel(pos_ref, freq_ref, cos_ref, sa_ref, sb_ref):
    ang = pos_ref[...] * freq_ref[...]
    lane = lax.broadcasted_iota(jnp.int32, ang.shape, 1)
    c, s = jnp.cos(ang), jnp.sin(ang)
    cos_ref[...] = jnp.where(lane < ROPE_DIM, c, 1.0)
    sa_ref[...] = jnp.where(lane < ROPE_HALF, -s, 0.0)
    sb_ref[...] = jnp.where((lane >= ROPE_HALF) & (lane < ROPE_DIM), s, 0.0)


def rope_tables(positions):
    s = positions.shape[0]
    inv_freq = ROPE_THETA ** (-jnp.arange(ROPE_HALF, dtype=F32) / ROPE_HALF)
    freq_row = jnp.zeros((HEAD_DIM,), F32).at[:ROPE_DIM].set(jnp.concatenate([inv_freq, inv_freq]))
    pos_rep = jnp.broadcast_to(positions.astype(F32)[:, None], (s, HEAD_DIM))
    tm = _tile(s, 1024)
    spec = pl.BlockSpec((tm, HEAD_DIM), lambda i: (i, 0))
    shp = jax.ShapeDtypeStruct((s, HEAD_DIM), F32)
    return pl.pallas_call(
        _rope_table_kernel,
        grid=(s // tm,),
        in_specs=[spec, pl.BlockSpec((1, HEAD_DIM), lambda i: (0, 0))],
        out_specs=[spec, spec, spec],
        out_shape=[shp, shp, shp],
        compiler_params=_params(("parallel",)),
        name="rope_tables",
    )(pos_rep, freq_row.reshape(1, HEAD_DIM))


def _gelu_tanh(x):
    return 0.5 * x * (1.0 + jnp.tanh(np.sqrt(2.0 / np.pi).astype(np.float32) * (x + 0.044715 * (x * x * x))))


def _compress_kernel(a_ref, w_ref, pe_ref, w2_ref, g_ref, o_ref, acc_ref, bias_ref, *, norm):
    l = pl.program_id(0)
    nl = pl.num_programs(0)
    a = a_ref[...]
    w = w_ref[0]
    pe_part = jnp.dot(pe_ref[0], w, preferred_element_type=F32)

    @pl.when(l == 0)
    def _():
        bias_ref[...] = pe_part
        for g in range(NSA_KV_HEADS):
            acc_ref[g] = jnp.dot(a[:, g * HEAD_DIM:(g + 1) * HEAD_DIM], w, preferred_element_type=F32)

    @pl.when(l > 0)
    def _():
        bias_ref[...] += pe_part
        for g in range(NSA_KV_HEADS):
            acc_ref[g] += jnp.dot(a[:, g * HEAD_DIM:(g + 1) * HEAD_DIM], w, preferred_element_type=F32)

    @pl.when(l == nl - 1)
    def _():
        n_chunk = a.shape[0]
        bias = bias_ref[0:1, :HEAD_DIM] + bias_ref[1:2, HEAD_DIM:]
        for g in range(NSA_KV_HEADS):
            p = acc_ref[g]
            hid = p[:, :HEAD_DIM] + pltpu.roll(p[:, HEAD_DIM:], n_chunk - 1, 0) + bias
            out = jnp.dot(_gelu_tanh(hid).astype(BF16), w2_ref[...], preferred_element_type=F32)
            if norm:
                ms = jnp.mean(out * out, axis=-1, keepdims=True)
                out = out * lax.rsqrt(ms + NORM_EPS) * g_ref[...]
            o_ref[:, g * HEAD_DIM:(g + 1) * HEAD_DIM] = out.astype(o_ref.dtype)


def compress(kv, pe, w1, w2, gain):
    s = kv.shape[0]
    n_chunk = s // CMP_STRIDE
    a = kv.reshape(n_chunk, CMP_STRIDE * KV_WIDTH)
    r = CMP_LEN // CMP_STRIDE
    w1r = w1.reshape(r, CMP_STRIDE, HEAD_DIM, HEAD_DIM)
    wcat = jnp.concatenate([w1r[0], w1r[1]], axis=-1).astype(BF16)
    pe_r = pe.reshape(r, CMP_STRIDE, HEAD_DIM).transpose(1, 0, 2)
    pe_l = jnp.zeros((CMP_STRIDE, 8, HEAD_DIM), F32).at[:, :r].set(pe_r).astype(BF16)
    norm = gain is not None
    g = (gain if norm else jnp.ones((HEAD_DIM,), F32)).reshape(1, HEAD_DIM)
    return pl.pallas_call(
        functools.partial(_compress_kernel, norm=norm),
        grid=(CMP_STRIDE,),
        in_specs=[
            pl.BlockSpec((n_chunk, KV_WIDTH), lambda l: (0, l)),
            pl.BlockSpec((1, HEAD_DIM, 2 * HEAD_DIM), lambda l: (l, 0, 0)),
            pl.BlockSpec((1, 8, HEAD_DIM), lambda l: (l, 0, 0)),
            pl.BlockSpec((HEAD_DIM, HEAD_DIM), lambda l: (0, 0)),
            pl.BlockSpec((1, HEAD_DIM), lambda l: (0, 0)),
        ],
        out_specs=pl.BlockSpec((n_chunk, KV_WIDTH), lambda l: (0, 0)),
        out_shape=jax.ShapeDtypeStruct((n_chunk, KV_WIDTH), BF16),
        scratch_shapes=[pltpu.VMEM((NSA_KV_HEADS, n_chunk, 2 * HEAD_DIM), F32), pltpu.VMEM((8, 2 * HEAD_DIM), F32)],
        compiler_params=_params(("arbitrary",)),
        name="nsa_compress",
    )(a, wcat, pe_l, w2.astype(BF16), g)


def _stack_heads(qb):
    return jnp.concatenate([qb[:, j * HEAD_DIM:(j + 1) * HEAD_DIM] for j in range(NSA_GROUP)], axis=0)


def _cmp_body(i, q_ref, k_ref, v_ref, ov_ref, o_ref, sel_ref, width, nbw):
    tq = q_ref.shape[0]
    nb = sel_ref.shape[-1]
    q4 = _stack_heads(q_ref[...])
    s = lax.dot_general(q4, k_ref[:width, :], (((1,), (1,)), ((), ())), preferred_element_type=F32)
    rows = lax.broadcasted_iota(jnp.int32, (NSA_GROUP * tq, width), 0)
    cols = lax.broadcasted_iota(jnp.int32, (NSA_GROUP * tq, width), 1)
    tpos = i * tq + rows % tq
    s = jnp.where(cols * CMP_STRIDE + (CMP_LEN - 1) <= tpos, s, NEG_INF)
    m = jnp.max(s, axis=-1, keepdims=True)
    m = jnp.where(m == NEG_INF, 0.0, m)
    e = jnp.exp2(s - m)
    p = e * (1.0 / jnp.maximum(jnp.sum(e, axis=-1, keepdims=True), 1e-30))
    o = jnp.dot(p.astype(BF16), v_ref[:width, :], preferred_element_type=F32)
    for j in range(NSA_GROUP):
        o_ref[:, j * HEAD_DIM:(j + 1) * HEAD_DIM] = o[j * tq:(j + 1) * tq]

    psum = p[0:tq] + p[tq:2 * tq] + p[2 * tq:3 * tq] + p[3 * tq:4 * tq]
    p_hi = psum.astype(BF16)
    p_lo = (psum - p_hi.astype(F32)).astype(BF16)
    ov = ov_ref[:width, :nbw]
    imp = jnp.dot(p_hi, ov, preferred_element_type=F32) + jnp.dot(p_lo, ov, preferred_element_type=F32)

    blk = lax.broadcasted_iota(jnp.int32, (tq, nbw), 1)
    t = i * tq + lax.broadcasted_iota(jnp.int32, (tq, nbw), 0)
    cur = t // SLC_LEN
    forced = (blk == 0) | (blk == cur) | (blk == cur - 1)
    valid = blk * SLC_LEN <= t
    work = jnp.where(forced, jnp.inf, jnp.where(valid, imp, NEG_INF))
    sel = jnp.zeros((tq, nbw), F32)
    blk_f = blk.astype(F32)
    for _ in range(min(SLC_TOPK, nb)):
        mx = jnp.max(work, axis=-1, keepdims=True)
        first = jnp.min(jnp.where(work == mx, blk_f, float(nb)), axis=-1, keepdims=True)
        pick = (blk_f == first) & (mx > NEG_INF)
        sel = jnp.where(pick, 1.0, sel)
        work = jnp.where(pick, NEG_INF, work)
    bias = jnp.where(sel > 0.5, 0.0, SEL_MASK)
    if nbw < nb:
        bias = jnp.concatenate([bias, jnp.full((tq, nb - nbw), SEL_MASK, F32)], axis=1)
    sel_ref[...] = bias.astype(sel_ref.dtype)


def _cmp_kernel(q_ref, k_ref, v_ref, ov_ref, o_ref, sel_ref, *, col_chunk):
    i = pl.program_id(1)
    tq = q_ref.shape[0]
    n_chunk = k_ref.shape[0]
    nb = sel_ref.shape[-1]
    n_var = n_chunk // col_chunk
    need = (i * tq + tq - CMP_LEN) // CMP_STRIDE + 1
    variant = jnp.minimum((need - 1) // col_chunk, n_var - 1)
    for v in range(n_var):
        width = (v + 1) * col_chunk
        i_max = (CMP_STRIDE * width + CMP_LEN - 1) // tq - 1
        blocks = ((i_max + 1) * tq - 1) // SLC_LEN + 1
        nbw = min(nb, -(-blocks // LANES) * LANES)

        @pl.when(variant == v)
        def _(width=width, nbw=nbw):
            _cmp_body(i, q_ref, k_ref, v_ref, ov_ref, o_ref, sel_ref, width, nbw)


def nsa_cmp_and_select(q, k_cmp, v_cmp):
    s = q.shape[0]
    n_chunk = k_cmp.shape[0]
    nb = s // SLC_LEN
    tq = _tile(s, CMP_Q_TILE)
    c_start = np.arange(n_chunk) * CMP_STRIDE
    s_start = np.arange(nb) * SLC_LEN
    overlap = np.clip(np.minimum(c_start[:, None] + CMP_LEN, s_start[None, :] + SLC_LEN)
                      - np.maximum(c_start[:, None], s_start[None, :]), 0, None).astype(np.float32) / CMP_LEN
    gw = NSA_GROUP * HEAD_DIM
    col_chunk = max(LANES, n_chunk // CMP_WIDTH_VARIANTS)
    return pl.pallas_call(
        functools.partial(_cmp_kernel, col_chunk=col_chunk),
        grid=(NSA_KV_HEADS, s // tq),
        in_specs=[
            pl.BlockSpec((tq, gw), lambda g, i: (i, g)),
            pl.BlockSpec((n_chunk, HEAD_DIM), lambda g, i: (0, g)),
            pl.BlockSpec((n_chunk, HEAD_DIM), lambda g, i: (0, g)),
            pl.BlockSpec((n_chunk, nb), lambda g, i: (0, 0)),
        ],
        out_specs=[
            pl.BlockSpec((tq, gw), lambda g, i: (i, g)),
            pl.BlockSpec((None, tq, nb), lambda g, i: (g, i, 0)),
        ],
        out_shape=[jax.ShapeDtypeStruct((s, NSA_WIDTH), F32), jax.ShapeDtypeStruct((NSA_KV_HEADS, s, nb), BF16)],
        compiler_params=_params(("parallel", "parallel")),
        name="nsa_cmp_select",
    )(q, k_cmp, v_cmp, jnp.asarray(overlap, BF16))


def _flash_tile(q2, k2, v, m_ref, acc_ref, mask=None):
    tk = k2.shape[0]
    s = lax.dot_general(q2, k2, (((1,), (1,)), ((), ())), preferred_element_type=F32)
    if mask is not None:
        s = jnp.where(mask, s, NEG_INF)
    m_old = m_ref[...]
    m_new = jnp.maximum(m_old, jnp.max(s, axis=-1, keepdims=True))
    alpha = jnp.exp2(m_old - m_new)
    p = jnp.exp2(s - jnp.concatenate([m_new] * (tk // LANES), axis=1))
    v2 = jnp.concatenate([v, jnp.ones((tk, LANES), v.dtype)], axis=1)
    acc_ref[...] = jnp.concatenate([alpha, alpha], axis=1) * acc_ref[...] + jnp.dot(
        p.astype(BF16), v2, preferred_element_type=F32)
    m_ref[...] = m_new


def _init_flash(m_ref, acc_ref):
    m_ref[...] = jnp.full(m_ref.shape, NEG_INF, F32)
    acc_ref[...] = jnp.zeros(acc_ref.shape, F32)


def _flash_result(acc_ref):
    acc = acc_ref[...]
    return acc[:, :HEAD_DIM] * (1.0 / acc[:, HEAD_DIM:])


def _for_each_tile(n, tile_fn):
    def body(jj, carry):
        for u in range(KV_UNROLL):
            tile_fn(jj * KV_UNROLL + u)
        return carry

    lax.fori_loop(0, n // KV_UNROLL, body, 0)
    base = (n // KV_UNROLL) * KV_UNROLL
    g = KV_UNROLL // 2
    while g >= 1:
        take = (n & g) != 0

        @pl.when(take)
        def _(base=base, g=g):
            for u in range(g):
                tile_fn(base + u)

        base = base + jnp.where(take, g, 0)
        g //= 2


def _slc_kernel(q_ref, bias_ref, k_ref, kx_ref, v_ref, o_ref, q2_ref, m_ref, acc_ref, *, tk):
    i = pl.program_id(1)
    tq = q_ref.shape[0]
    cw = kx_ref.shape[1]
    _init_flash(m_ref, acc_ref)
    q4 = _stack_heads(q_ref[...])
    for c in range(q2_ref.shape[0]):
        bias = bias_ref[:, c * cw:(c + 1) * cw]
        q2_ref[c] = jnp.concatenate([q4, jnp.concatenate([bias] * NSA_GROUP, axis=0)], axis=1)

    def tile(j, masked):
        off = pl.multiple_of(j * tk, tk)
        k2 = jnp.concatenate([k_ref[pl.ds(off, tk), :], kx_ref[pl.ds(off, tk), :]], axis=1)
        q2 = q2_ref[(j * tk) // (SLC_LEN * cw)]
        mask = None
        if masked:
            rows = lax.broadcasted_iota(jnp.int32, (NSA_GROUP * tq, tk), 0)
            cols = lax.broadcasted_iota(jnp.int32, (NSA_GROUP * tq, tk), 1)
            mask = off + cols <= i * tq + rows % tq
        _flash_tile(q2, k2, v_ref[pl.ds(off, tk), :], m_ref, acc_ref, mask)

    last = (i * tq) // tk
    _for_each_tile(last, lambda j: tile(j, False))
    tile(last, True)
    o = _flash_result(acc_ref)
    for h in range(NSA_GROUP):
        o_ref[:, h * HEAD_DIM:(h + 1) * HEAD_DIM] = o[h * tq:(h + 1) * tq]


def nsa_selected(q, k_slc, v_slc, sel_bias):
    s = q.shape[0]
    nb = s // SLC_LEN
    cw = min(nb, LANES)
    tq, tk = _tile(s, SLC_Q_TILE), _tile(s, 512)
    key_blk = jnp.arange(s, dtype=jnp.int32)[:, None] // SLC_LEN
    key_onehot = (key_blk % cw == jnp.arange(cw, dtype=jnp.int32)[None, :]).astype(BF16)
    gw = NSA_GROUP * HEAD_DIM
    rows = NSA_GROUP * tq
    return pl.pallas_call(
        functools.partial(_slc_kernel, tk=tk),
        grid=(NSA_KV_HEADS, s // tq),
        in_specs=[
            pl.BlockSpec((tq, gw), lambda g, i: (i, g)),
            pl.BlockSpec((None, tq, nb), lambda g, i: (g, i, 0)),
            pl.BlockSpec((s, HEAD_DIM), lambda g, i: (0, g)),
            pl.BlockSpec((s, cw), lambda g, i: (0, 0)),
            pl.BlockSpec((s, HEAD_DIM), lambda g, i: (0, g)),
        ],
        out_specs=pl.BlockSpec((tq, gw), lambda g, i: (i, g)),
        out_shape=jax.ShapeDtypeStruct((s, NSA_WIDTH), F32),
        scratch_shapes=[pltpu.VMEM((nb // cw, rows, HEAD_DIM + cw), BF16), pltpu.VMEM((rows, LANES), F32),
                        pltpu.VMEM((rows, 2 * HEAD_DIM), F32)],
        compiler_params=_params(("parallel", "arbitrary")),
        name="nsa_selected",
    )(q, sel_bias, k_slc, key_onehot, v_slc)


WIN_TILES = WIN_LEN // Q_BLOCK + 1


def _win_kernel(*refs):
    q_ref = refs[0]
    k_refs = refs[1:1 + WIN_TILES]
    v_refs = refs[1 + WIN_TILES:1 + 2 * WIN_TILES]
    ocmp_ref, oslc_ref, gate_ref, o_ref = refs[1 + 2 * WIN_TILES:]
    i = pl.program_id(0)
    tq = q_ref.shape[0]
    q4 = _stack_heads(q_ref[...])
    kcat = jnp.concatenate([r[...] for r in k_refs], axis=0)
    vcat = jnp.concatenate([r[...] for r in v_refs], axis=0)
    s = lax.dot_general(q4, kcat, (((1,), (1,)), ((), ())), preferred_element_type=F32)
    rows = lax.broadcasted_iota(jnp.int32, s.shape, 0)
    cols = lax.broadcasted_iota(jnp.int32, s.shape, 1)
    tpos = i * tq + rows % tq
    wpos = (i - (WIN_TILES - 1)) * tq + cols
    ok = (wpos <= tpos) & (wpos > tpos - WIN_LEN) & (wpos >= 0)
    s = jnp.where(ok, s, NEG_INF)
    m = jnp.max(s, axis=-1, keepdims=True)
    e = jnp.exp2(s - m)
    p = e * (1.0 / jnp.sum(e, axis=-1, keepdims=True))
    o_win = jnp.dot(p.astype(BF16), vcat, preferred_element_type=F32)
    gates = gate_ref[...]
    for h in range(NSA_GROUP):
        sl = slice(h * HEAD_DIM, (h + 1) * HEAD_DIM)
        o = (gates[:, 3 * h:3 * h + 1] * ocmp_ref[:, sl] + gates[:, 3 * h + 1:3 * h + 2] * oslc_ref[:, sl]
             + gates[:, 3 * h + 2:3 * h + 3] * o_win[h * tq:(h + 1) * tq])
        o_ref[:, sl] = o.astype(o_ref.dtype)


def nsa_window_combine(q, k_win, v_win, o_cmp, o_slc, gates):
    s = q.shape[0]
    tq = Q_BLOCK
    gw = NSA_GROUP * HEAD_DIM
    back = WIN_TILES - 1
    kv_specs = [pl.BlockSpec((tq, HEAD_DIM), lambda i, g, d=d: (jnp.maximum(i - back + d, 0), g))
                for d in range(WIN_TILES)]
    blk = pl.BlockSpec((tq, gw), lambda i, g: (i, g))
    return pl.pallas_call(
        _win_kernel,
        grid=(s // tq, NSA_KV_HEADS),
        in_specs=[blk] + kv_specs + kv_specs + [blk, blk, pl.BlockSpec((None, tq, 3 * NSA_GROUP), lambda i, g: (g, i, 0))],
        out_specs=blk,
        out_shape=jax.ShapeDtypeStruct((s, NSA_WIDTH), BF16),
        compiler_params=_params(("parallel", "parallel")),
        name="nsa_window_combine",
    )(q, *([k_win] * WIN_TILES), *([v_win] * WIN_TILES), o_cmp, o_slc, gates)


def _conv_kernel(b_ref, c_ref, h_ref, ch_ref, hh_ref, w_ref, o_ref):
    i = pl.program_id(0)
    u = c_ref[...] * h_ref[...]
    halo = jnp.where(i > 0, ch_ref[...] * hh_ref[...], 0.0)
    x = jnp.concatenate([halo, u], axis=0)
    w = w_ref[...]
    y = (w[2:3] * x + w[1:2] * pltpu.roll(x, 1, 0) + w[0:1] * pltpu.roll(x, 2, 0))[CONV_HALO:]
    o_ref[...] = (b_ref[...] * y).astype(o_ref.dtype)


def short_conv(z, conv_w):
    s = z.shape[0]
    cw = conv_w.shape[1]
    tm, tc = _tile(s, 512), _tile(cw, 512)
    nc = cw // tc
    hb = tm // CONV_HALO
    halo = lambda off: pl.BlockSpec((CONV_HALO, tc), lambda i, j: (jnp.maximum(i * hb - 1, 0), off * nc + j))
    main = lambda off: pl.BlockSpec((tm, tc), lambda i, j: (i, off * nc + j))
    w8 = jnp.zeros((8, cw), F32).at[:CONV_K].set(conv_w)
    return pl.pallas_call(
        _conv_kernel,
        grid=(s // tm, nc),
        in_specs=[main(0), main(1), main(2), halo(1), halo(2), pl.BlockSpec((8, tc), lambda i, j: (0, j))],
        out_specs=pl.BlockSpec((tm, tc), lambda i, j: (i, j)),
        out_shape=jax.ShapeDtypeStruct((s, cw), BF16),
        compiler_params=_params(("parallel", "parallel")),
        name="short_conv",
    )(z, z, z, z, z, w8)


def _split3(x):
    hi = x.astype(BF16)
    r1 = x - hi.astype(F32)
    mid = r1.astype(BF16)
    lo = (r1 - mid.astype(F32)).astype(BF16)
    return hi, mid, lo


def _cumsum_kernel(x_ref, hi_ref, mid_ref, lo_ref, carry_ref):
    @pl.when(pl.program_id(0) == 0)
    def _():
        carry_ref[...] = jnp.zeros(carry_ref.shape, F32)

    x = x_ref[...]
    t = x.shape[0]
    tri = (lax.broadcasted_iota(jnp.int32, (t, t), 0) >= lax.broadcasted_iota(jnp.int32, (t, t), 1)).astype(BF16)
    c = sum(jnp.dot(tri, part, preferred_element_type=F32) for part in _split3(x)) + carry_ref[0:1]
    carry_ref[...] = jnp.broadcast_to(c[t - 1:t], carry_ref.shape)
    hi_ref[...], mid_ref[...], lo_ref[...] = _split3(c * LOG2E)


def cumsum_log2_split(x):
    s, w = x.shape
    t = _tile(s, 512)
    spec = pl.BlockSpec((t, w), lambda i: (i, 0))
    shp = jax.ShapeDtypeStruct((s, w), BF16)
    return pl.pallas_call(
        _cumsum_kernel,
        grid=(s // t,),
        in_specs=[spec],
        out_specs=[spec, spec, spec],
        out_shape=[shp, shp, shp],
        scratch_shapes=[pltpu.VMEM((8, w), F32)],
        compiler_params=_params(("arbitrary",)),
        name="cumsum",
    )(x)


def _fox_kernel(q_ref, qx_ref, k_ref, kx_ref, v_ref, cq_ref, ck_ref, o_ref, m_ref, acc_ref, kn_ref, *, t):
    i = pl.program_id(1)
    _init_flash(m_ref, acc_ref)
    q = q_ref[...]
    q2 = jnp.concatenate([q, qx_ref[...]], axis=1)

    @pl.when(i == 0)
    def _():
        def body(j, mx):
            kk = k_ref[pl.ds(pl.multiple_of(j * t, t), t), :].astype(F32)
            return jnp.maximum(mx, jnp.max(jnp.sum(kk * kk, axis=1, keepdims=True), axis=0, keepdims=True))
        kn_ref[...] = jnp.broadcast_to(lax.fori_loop(0, k_ref.shape[0] // t, body, jnp.zeros((1, 1), F32)), kn_ref.shape)

    qq = q.astype(F32)
    bound = jnp.max(jnp.sqrt(jnp.sum(qq * qq, axis=1, keepdims=True) * kn_ref[0:1, 0:1]), axis=0, keepdims=True)
    gap = cq_ref[pl.ds(i, 1), :] - ck_ref[...]
    lane = lax.broadcasted_iota(jnp.int32, gap.shape, 1)
    keep = (gap >= -(2.0 * bound + FOX_PRUNE_BITS)) & (lane < i)
    n_keep = jnp.sum(jnp.where(keep, 1, 0))
    first = i - n_keep

    def tile(j, masked):
        off = pl.multiple_of(j * t, t)
        k2 = jnp.concatenate([k_ref[pl.ds(off, t), :], kx_ref[pl.ds(off, t), :]], axis=1)
        mask = None
        if masked:
            mask = lax.broadcasted_iota(jnp.int32, (t, t), 1) <= lax.broadcasted_iota(jnp.int32, (t, t), 0)
        _flash_tile(q2, k2, v_ref[pl.ds(off, t), :], m_ref, acc_ref, mask)

    _for_each_tile(n_keep, lambda jj: tile(first + jj, False))
    tile(i, True)
    o_ref[...] = _flash_result(acc_ref).astype(o_ref.dtype)


def fox_attention(q, k, v, c_parts):
    s = q.shape[0]
    t = _tile(s, 512)
    parts = jnp.stack([part[:, :FOX_HEADS] for part in c_parts], axis=-1)
    ones = jnp.ones_like(parts)
    widen = lambda a: jnp.pad(a, ((0, 0), (0, 0), (0, HEAD_DIM - a.shape[-1]))).reshape(s, FOX_WIDTH)
    qx = widen(jnp.concatenate([parts, ones], axis=-1))
    kx = widen(jnp.concatenate([ones, -parts], axis=-1))
    nt = s // t
    assert nt <= LANES
    c = sum(part[:, :FOX_HEADS].astype(F32) for part in c_parts)
    cq_first = jnp.broadcast_to(c[0::t].T[:, :, None], (FOX_HEADS, nt, LANES))
    ck_last = jnp.pad(c[t - 1::t].T, ((0, 0), (0, LANES - nt))).reshape(FOX_HEADS, 1, LANES)
    tile_spec = pl.BlockSpec((t, HEAD_DIM), lambda h, i: (i, h))
    full_spec = pl.BlockSpec((s, HEAD_DIM), lambda h, i: (0, h))
    return pl.pallas_call(
        functools.partial(_fox_kernel, t=t),
        grid=(FOX_HEADS, nt),
        in_specs=[tile_spec, tile_spec, full_spec, full_spec, full_spec,
                  pl.BlockSpec((None, nt, LANES), lambda h, i: (h, 0, 0)),
                  pl.BlockSpec((None, 1, LANES), lambda h, i: (h, 0, 0))],
        out_specs=tile_spec,
        out_shape=jax.ShapeDtypeStruct((s, FOX_WIDTH), BF16),
        scratch_shapes=[pltpu.VMEM((t, LANES), F32), pltpu.VMEM((t, 2 * HEAD_DIM), F32), pltpu.VMEM((8, LANES), F32)],
        compiler_params=_params(("parallel", "arbitrary")),
        name="fox_attention",
    )(q, qx, k, kx, v, cq_first, ck_last)


def _pool_kernel(u_ref, halo_ref, w_ref, scale_ref, o_ref):
    i = pl.program_id(0)
    tm = u_ref.shape[0]
    u = u_ref[...]
    halo = jnp.where(i > 0, halo_ref[...], 0.0)
    x = jnp.concatenate([halo, u], axis=0)
    t1 = i * tm + lax.broadcasted_iota(jnp.int32, (tm, POOL_GROUP), 0) + 1
    for g, win in enumerate(POOL_WINDOWS):
        sl = slice(g * POOL_GROUP, (g + 1) * POOL_GROUP)
        acc = x[:, sl]
        span = 1
        while span < win:
            acc = acc + pltpu.roll(acc, span, 0)
            span *= 2
        cnt = jnp.minimum(t1, win).astype(F32)
        d = acc[POOL_HALO:] / cnt - u[:, sl]
        y = jnp.dot(d.astype(BF16), w_ref[g], preferred_element_type=F32)
        o_ref[:, sl] = (y * scale_ref[:, sl]).astype(o_ref.dtype)


def multiscale_pool(u, w_pool, scale):
    s, pw = u.shape
    tm = _tile(s, 512)
    hb = tm // POOL_HALO
    return pl.pallas_call(
        _pool_kernel,
        grid=(s // tm,),
        in_specs=[
            pl.BlockSpec((tm, pw), lambda i: (i, 0)),
            pl.BlockSpec((POOL_HALO, pw), lambda i: (jnp.maximum(i * hb - 1, 0), 0)),
            pl.BlockSpec(w_pool.shape, lambda i: (0, 0, 0)),
            pl.BlockSpec((1, pw), lambda i: (0, 0)),
        ],
        out_specs=pl.BlockSpec((tm, pw), lambda i: (i, 0)),
        out_shape=jax.ShapeDtypeStruct((s, pw), BF16),
        compiler_params=_params(("parallel",)),
        name="multiscale_pool",
    )(u, u, w_pool.astype(BF16), scale.reshape(1, pw))


def _swiglu_kernel(h_ref, w1_ref, w3_ref, o_ref):
    h = h_ref[...]
    a = jnp.dot(h, w1_ref[...].astype(BF16), preferred_element_type=F32)
    b = jnp.dot(h, w3_ref[...].astype(BF16), preferred_element_type=F32)
    o_ref[...] = (a * jax.nn.sigmoid(a) * b).astype(o_ref.dtype)


def swiglu_up(h, w1, w3):
    m, d = h.shape
    n = w1.cols
    tm, tn = _tile(m, 2048), _tile(n, 512)
    specs = [w.spec(d, tn, lambda i, j: 0, lambda i, j: j) for w in (w1, w3)]
    return pl.pallas_call(
        _swiglu_kernel,
        grid=(m // tm, n // tn),
        in_specs=[_lhs_spec((tm, d), lambda i, j: (i, 0), True)] + specs,
        out_specs=pl.BlockSpec((tm, tn), lambda i, j: (i, j)),
        out_shape=jax.ShapeDtypeStruct((m, n), BF16),
        compiler_params=_params(("parallel", "parallel")),
        name="swiglu_up",
    )(h, w1.arr, w3.arr)


def _resid_epilogue(acc, x_ref):
    return x_ref[...] + acc


def _gate_epilogue(acc, x_ref, p_ref, wp_ref):
    return x_ref[...] + jax.nn.sigmoid(acc) * jnp.dot(p_ref[...], wp_ref[...].astype(BF16),
                                                      preferred_element_type=F32)


def _logsig_epilogue(acc, b_ref):
    y = -(acc + b_ref[...])
    return -(jnp.maximum(y, 0.0) + jnp.log1p(jnp.exp(-jnp.abs(y))))


def _sigmoid_epilogue(acc):
    return jax.nn.sigmoid(acc)


def matmul_resid(a, b, x, *, tk=4096, name):
    tm, tn = _tile(a.shape[0], 1024), _tile(b.cols, 512)
    return matmul(a, b, out_dtype=F32, epilogue=_resid_epilogue, extras=[_tile_extra(x, tm, tn)], tm=tm, tn=tn, tk=tk,
                  name=name)


def _out_proj_kernel(a1_ref, a2_ref, w1_ref, w2_ref, x_ref, o_ref):
    acc = jnp.dot(a1_ref[...], w1_ref[...], preferred_element_type=F32)
    acc += jnp.dot(a2_ref[...], w2_ref[...], preferred_element_type=F32)
    o_ref[...] = x_ref[...] + acc


def out_proj_resid(a1, a2, w, x, *, name):
    m, k1 = a1.shape
    k2 = a2.shape[1]
    assert k1 % k2 == 0 and w.k == k1 + k2
    n = w.cols
    tm, tn = _tile(m, 1024), _tile(n, 512)
    ij = lambda i, j: (i, j)
    return pl.pallas_call(
        _out_proj_kernel,
        grid=(m // tm, n // tn),
        in_specs=[pl.BlockSpec((tm, k1), lambda i, j: (i, 0)), pl.BlockSpec((tm, k2), lambda i, j: (i, 0)),
                  w.spec(k1, tn, lambda i, j: 0, lambda i, j: j), w.spec(k2, tn, lambda i, j: k1 // k2, lambda i, j: j),
                  pl.BlockSpec((tm, tn), ij)],
        out_specs=pl.BlockSpec((tm, tn), ij),
        out_shape=jax.ShapeDtypeStruct((m, n), F32),
        compiler_params=_params(("parallel", "parallel")),
        name=name,
    )(a1, a2, w.arr, w.arr, x)


def _k_tile(k, cap):
    for t in range(cap - cap % LANES, 0, -LANES):
        if k % t == 0:
            return t
    return k


FFN_DOWN_TK_CAP = 5632


def ffn_and_embed(x, p_i, layer, f_norm, w1, w3, w2, e_norm, w_gate, w_proj):
    h = rmsnorm_bf16(x, f_norm)
    u = swiglu_up(h, Weight(w1, layer), Weight(w3, layer))
    x = matmul_resid(u, Weight(w2, layer), x, tk=_k_tile(w2.shape[-2], FFN_DOWN_TK_CAP), name="ffn_down")
    h = rmsnorm_bf16(x, e_norm)
    m, d = x.shape
    tm, tn = _tile(m, 1024), _tile(d, 512)
    pd = p_i.shape[1]
    extras = [_tile_extra(x, tm, tn), (p_i.astype(BF16), (tm, pd), lambda i, j, k: (i, 0)),
              (w_proj, (None, pd, tn), lambda i, j, k: (layer, 0, j))]
    return matmul(h, Weight(w_gate, layer), out_dtype=F32, epilogue=_gate_epilogue, extras=extras, tm=tm, tn=tn,
                  name="embed_gate")


def even_mixer(x, rope, layer, norm_g, w_in, q_norm, k_norm, cmp_pe, cmp_w1, cmp_w2, conv_w, w_out):
    s = x.shape[0]
    h = rmsnorm_bf16(x, norm_g)
    o_kv = NSA_WIDTH
    o_gate = o_kv + 6 * KV_WIDTH
    o_conv = o_gate + 3 * NSA_HEADS
    kv_cols = lambda i: slice(o_kv + i * KV_WIDTH, o_kv + (i + 1) * KV_WIDTH)
    tile_gain = lambda g, n: jnp.tile(g, n // HEAD_DIM)

    q = head_proj(h, Weight(w_in, layer, 0, NSA_WIDTH), gain=tile_gain(q_norm, NSA_WIDTH), rope=rope, scale=Q_SCALE,
                  name="nsa_q_proj")
    k_cmp_in = head_proj(h, Weight(w_in, layer, o_kv, KV_WIDTH), rope=rope, name="nsa_kcmp_proj")
    w_v = jnp.concatenate([w_in[layer, :, kv_cols(i)] for i in (1, 3, 5)], axis=1)
    v_all = matmul(h, w_v, out_dtype=BF16, name="nsa_v_proj")
    v_cmp_in, v_slc, v_win = (v_all[:, i * KV_WIDTH:(i + 1) * KV_WIDTH] for i in range(3))
    w_k = jnp.concatenate([w_in[layer, :, kv_cols(i)] for i in (2, 4)], axis=1)
    g_k = jnp.concatenate([tile_gain(k_norm[1], KV_WIDTH), tile_gain(k_norm[2], KV_WIDTH)])
    k_both = head_proj(h, Weight(w_k), gain=g_k, rope=rope, name="nsa_k_proj")
    k_slc, k_win = k_both[:, :KV_WIDTH], k_both[:, KV_WIDTH:]
    gates = matmul(h, Weight(w_in, layer, o_gate, LANES), out_dtype=F32, epilogue=_sigmoid_epilogue,
                   name="nsa_gate_proj")[:, :3 * NSA_HEADS]
    gates = gates.reshape(s, NSA_KV_HEADS, 3 * NSA_GROUP).transpose(1, 0, 2)
    z_conv = matmul(h, w_in[layer, :, o_conv:], out_dtype=F32, name="conv_proj")

    k_cmp = compress(k_cmp_in, cmp_pe[0], cmp_w1[0], cmp_w2[0], k_norm[0])
    v_cmp = compress(v_cmp_in, cmp_pe[1], cmp_w1[1], cmp_w2[1], None)
    o_cmp, sel = nsa_cmp_and_select(q, k_cmp, v_cmp)
    o_slc = nsa_selected(q, k_slc, v_slc, sel)
    o_nsa = nsa_window_combine(q, k_win, v_win, o_cmp, o_slc, gates)
    o_conv_out = short_conv(z_conv, conv_w)
    return out_proj_resid(o_nsa, o_conv_out, Weight(w_out, layer), x, name="even_out_proj")


def odd_mixer(x, layer, norm_g, w_in, f_bias, q_norm, k_norm, pool_w, pool_scale, w_out):
    h = rmsnorm_bf16(x, norm_g)
    o_fgate = 3 * FOX_WIDTH
    o_pool = o_fgate + FOX_HEADS
    tile_gain = lambda g: jnp.tile(g, FOX_WIDTH // HEAD_DIM)
    q = head_proj(h, Weight(w_in, layer, 0, FOX_WIDTH), gain=tile_gain(q_norm), scale=Q_SCALE, name="fox_q_proj")
    k = head_proj(h, Weight(w_in, layer, FOX_WIDTH, FOX_WIDTH), gain=tile_gain(k_norm), name="fox_k_proj")
    v = matmul(h, Weight(w_in, layer, 2 * FOX_WIDTH, FOX_WIDTH), out_dtype=BF16, name="fox_v_proj")
    b_f = jnp.pad(f_bias, (0, LANES - FOX_HEADS)).reshape(1, LANES)
    log_f = matmul(h, Weight(w_in, layer, o_fgate, LANES), out_dtype=F32, epilogue=_logsig_epilogue,
                   extras=[_row_extra(b_f, LANES)], name="fox_gate_proj")
    u = matmul(h, w_in[layer, :, o_pool:], out_dtype=F32, name="pool_proj")
    o_fox = fox_attention(q, k, v, cumsum_log2_split(log_f))
    o_pool_out = multiscale_pool(u, pool_w, pool_scale)
    return out_proj_resid(o_fox, o_pool_out, Weight(w_out, layer), x, name="odd_out_proj")


def kernel(x, p, positions, a_norm, a_w_in, a_q_norm, a_k_norm, a_cmp_pe, a_cmp_w1, a_cmp_w2, a_conv_w, a_w_out, b_norm, b_w_in, b_f_bias, b_q_norm, b_k_norm, b_pool_w, b_pool_scale, b_w_out, f_norm, f_w1, f_w3, f_w2, e_norm, e_w_gate, e_w_proj):
    batch, s, d = x.shape
    depth = p.shape[0]
    a_w_in, a_w_out, b_w_in, b_w_out, f_w2, e_w_gate, e_w_proj = (
        w.astype(BF16) for w in (a_w_in, a_w_out, b_w_in, b_w_out, f_w2, e_w_gate, e_w_proj))
    outs = []
    for b in range(batch):
        xb = x[b]
        rope = rope_tables(positions[b])
        for i in range(depth):
            j = i // 2
            if i % 2 == 0:
                xb = even_mixer(xb, rope, j, a_norm[j], a_w_in, a_q_norm[j], a_k_norm[j], a_cmp_pe[j], a_cmp_w1[j],
                                a_cmp_w2[j], a_conv_w[j], a_w_out)
            else:
                xb = odd_mixer(xb, j, b_norm[j], b_w_in, b_f_bias[j], b_q_norm[j], b_k_norm[j], b_pool_w[j],
                               b_pool_scale[j], b_w_out)
            xb = ffn_and_embed(xb, p[i, b], i, f_norm[i], f_w1, f_w3, f_w2, e_norm[i], e_w_gate, e_w_proj)
        outs.append(xb)
    return jnp.stack(outs, axis=0)
```
